```python
import math
import jax, jax.numpy as jnp
from jax import lax
import numpy as np

D_MODEL = 1024
BATCH = 8
SEQ = 2048
DEPTH = 4
DEC_BATCH = 32
DEC_SEQ = 1
PAST_LEN = 8192
PAGE_SIZE = 128

N_MIXERS = 3
EPS = 1e-6
NEG = -1e30
BIG = 1e9
TINY = 1e-30
HEAD_DIM = 64

NSA_HEADS = 16
NSA_KV_HEADS = 4
NSA_GROUP = NSA_HEADS // NSA_KV_HEADS
CMP_BLOCK = 32
CMP_STRIDE = 16
SEL_BLOCK = 64
SEL_TOP = 16
NSA_WINDOW = 512
PHI_HIDDEN = 128
NSA_QB = 32

D_RNN = D_MODEL
CONV_W = 4
LRU_BLOCKS = 4
LRU_BLOCK_W = D_RNN // LRU_BLOCKS
LRU_C = 8.0

DIL_PAIRS = ((128, 1), (512, 4), (2048, 16))
DIL_GROUPS = 3
DIL_HEADS = 8
DIL_QB = 32

N_EXPERT_GROUPS = 4
EXPERTS_PER_GROUP = 4
N_EXPERTS = N_EXPERT_GROUPS * EXPERTS_PER_GROUP
EXPERT_TOP = 2
D_EXPERT = 512

kernel_name = 'hybrid_nsa_rglru_dilated_hmoe_step'


def rms_norm(x, g):
    xf = x.astype(jnp.float32)
    y = xf * lax.rsqrt(jnp.mean(xf * xf, axis=-1, keepdims=True) + EPS)
    return (y * g.astype(jnp.float32)).astype(x.dtype)


def masked_softmax(s, mask):
    s = jnp.where(mask, s, NEG)
    e = jnp.exp(s - jnp.max(s, axis=-1, keepdims=True)) * mask
    return e / jnp.maximum(jnp.sum(e, axis=-1, keepdims=True), TINY)


def nsa_project(h, w_in, q_norm, k_norm):
    B, T, _ = h.shape
    qd = NSA_HEADS * HEAD_DIM
    proj = h @ w_in
    q = rms_norm(proj[..., :qd].reshape(B, T, NSA_HEADS, HEAD_DIM), q_norm) * (HEAD_DIM ** -0.5)
    gates = jax.nn.sigmoid(proj[..., qd:qd + 3 * NSA_HEADS].astype(jnp.float32)).reshape(B, T, NSA_HEADS, 3)
    kv = proj[..., qd + 3 * NSA_HEADS:].reshape(B, T, 6, NSA_KV_HEADS, HEAD_DIM)
    rows = jnp.stack([kv[:, :, 0], kv[:, :, 1], rms_norm(kv[:, :, 2], k_norm[1]), kv[:, :, 3]], axis=2)
    win = jnp.stack([rms_norm(kv[:, :, 4], k_norm[2]), kv[:, :, 5]], axis=2)
    return q, gates, rows, win


def compress_rows(r, w1, b1, w2, b2):
    B, L, KV, Dh = r.shape
    n_half = -(-L // CMP_STRIDE)
    r = jnp.pad(r, ((0, 0), (0, n_half * CMP_STRIDE - L), (0, 0), (0, 0))).reshape(B, n_half, CMP_STRIDE, KV, Dh)
    lead = jnp.einsum('bnskd,sdh->bnkh', r[:, :-1], w1[:CMP_STRIDE])
    tail = jnp.einsum('bnskd,sdh->bnkh', r[:, 1:], w1[CMP_STRIDE:])
    hid = jax.nn.gelu(lead + tail + b1)
    return jnp.einsum('bnkh,hd->bnkd', hid, w2) + b2


def nsa_attention(q, gates, rows, win, q_pos0, w_pos0, k_cmp_norm, cmp_w1, cmp_b1, cmp_w2, cmp_b2):
    B, Tq, H, Dh = q.shape
    L = rows.shape[1]
    kc = rms_norm(compress_rows(rows[:, :, 0], cmp_w1[0], cmp_b1[0], cmp_w2[0], cmp_b2[0]), k_cmp_norm)
    vc = compress_rows(rows[:, :, 1], cmp_w1[1], cmp_b1[1], cmp_w2[1], cmp_b2[1])
    n_cmp = kc.shape[1]
    n_sel = -(-L // SEL_BLOCK)
    sel_rows = jnp.pad(rows[:, :, 2:], ((0, 0), (0, n_sel * SEL_BLOCK - L), (0, 0), (0, 0), (0, 0)))
    sel_rows = sel_rows.reshape(B, n_sel, SEL_BLOCK, 2, NSA_KV_HEADS, Dh).transpose(0, 4, 1, 2, 3, 5)
    n_top = min(SEL_TOP, n_sel)
    win_pad = jnp.pad(win, ((0, 0), (NSA_WINDOW - 1, 0), (0, 0), (0, 0), (0, 0)))
    qb = math.gcd(Tq, NSA_QB)
    w_len = NSA_WINDOW + qb - 1
    ratio = SEL_BLOCK // CMP_STRIDE
    span = CMP_BLOCK // CMP_STRIDE
    map_idx = np.arange(n_sel)[:, None] * ratio - (span - 1) + np.arange(ratio + span - 1)[None, :]
    map_w = np.convolve(np.ones(ratio), np.ones(span))[None, :] * ((map_idx >= 0) & (map_idx < n_cmp))
    map_idx = jnp.asarray(np.clip(map_idx, 0, n_cmp - 1), jnp.int32)
    map_w = jnp.asarray(map_w, jnp.float32)
    cmp_end = jnp.arange(n_cmp) * CMP_STRIDE + CMP_BLOCK - 1
    blk = jnp.arange(n_sel)
    qg = q.reshape(B, Tq, NSA_KV_HEADS, NSA_GROUP, Dh)
    gg = gates.reshape(B, Tq, NSA_KV_HEADS, NSA_GROUP, 3)
    bi = jnp.arange(B)[:, None, None, None]
    ki = jnp.arange(NSA_KV_HEADS)[None, :, None, None]

    def one_block(start):
        qblk = lax.dynamic_slice_in_dim(qg, start, qb, axis=1)
        gblk = lax.dynamic_slice_in_dim(gg, start, qb, axis=1)
        t = q_pos0 + start + jnp.arange(qb)
        s_c = jnp.einsum('bqkgd,bnkd->bkgqn', qblk, kc).astype(jnp.float32)
        p_c = masked_softmax(s_c, cmp_end[None, :] <= t[:, None])
        o_c = jnp.einsum('bkgqn,bnkd->bqkgd', p_c, vc)
        imp = jnp.sum(p_c, axis=2)
        p_blk = jnp.sum(jnp.take(imp, map_idx, axis=-1) * map_w, axis=-1)
        cur = t[:, None] // SEL_BLOCK
        forced = (blk[None, :] == 0) | (blk[None, :] == cur) | (blk[None, :] == cur - 1)
        score = jnp.where(blk[None, :] * SEL_BLOCK > t[:, None], NEG, jnp.where(forced, BIG, p_blk))
        _, sel = lax.top_k(score, n_top)
        kv_sel = sel_rows[bi, ki, sel].reshape(B, NSA_KV_HEADS, qb, n_top * SEL_BLOCK, 2, Dh)
        pos = (sel[..., None] * SEL_BLOCK + jnp.arange(SEL_BLOCK)).reshape(B, NSA_KV_HEADS, qb, n_top * SEL_BLOCK)
        s_s = jnp.einsum('bqkgd,bkqmd->bkgqm', qblk, kv_sel[..., 0, :]).astype(jnp.float32)
        p_s = masked_softmax(s_s, (pos <= t[:, None])[:, :, None])
        o_s = jnp.einsum('bkgqm,bkqmd->bqkgd', p_s, kv_sel[..., 1, :])
        kw = lax.dynamic_slice_in_dim(win_pad, q_pos0 + start - w_pos0, w_len, axis=1)
        kpos = q_pos0 + start - (NSA_WINDOW - 1) + jnp.arange(w_len)
        dist = t[:, None] - kpos[None, :]
        wmask = (kpos[None, :] >= 0) & (dist >= 0) & (dist < NSA_WINDOW)
        s_w = jnp.einsum('bqkgd,bmkd->bkgqm', qblk, kw[:, :, 0]).astype(jnp.float32)
        p_w = masked_softmax(s_w, wmask)
        o_w = jnp.einsum('bkgqm,bmkd->bqkgd', p_w, kw[:, :, 1])
        return gblk[..., 0:1] * o_c + gblk[..., 1:2] * o_s + gblk[..., 2:3] * o_w

    out = lax.map(one_block, jnp.arange(Tq // qb) * qb)
    return jnp.moveaxis(out, 0, 1).reshape(B, Tq, H * Dh).astype(q.dtype)


def rglru_mixer(h, conv_buf, h0, w_in, conv_w, conv_b, ga_w, ga_b, gx_w, gx_b, lam, w_out):
    B, T, _ = h.shape
    xy = h @ w_in
    xb, yb = xy[..., :D_RNN], xy[..., D_RNN:]
    xpad = jnp.concatenate([conv_buf.astype(xb.dtype), xb], axis=1)
    xc = conv_b
    for k in range(CONV_W):
        xc = xc + xpad[:, k:k + T] * conv_w[k]
    xg = xc.reshape(B, T, LRU_BLOCKS, LRU_BLOCK_W)
    r = jax.nn.sigmoid((jnp.einsum('btnc,ncd->btnd', xg, ga_w).reshape(B, T, D_RNN) + ga_b).astype(jnp.float32))
    i = jax.nn.sigmoid((jnp.einsum('btnc,ncd->btnd', xg, gx_w).reshape(B, T, D_RNN) + gx_b).astype(jnp.float32))
    log_a = -LRU_C * r * jax.nn.softplus(-lam.astype(jnp.float32))
    a = jnp.exp(log_a)
    b = jnp.sqrt(-jnp.expm1(2.0 * log_a)) * (i * xc.astype(jnp.float32))
    b = b.at[:, 0].add(a[:, 0] * h0.astype(jnp.float32))

    def combine(left, right):
        return (left[0] * right[0], right[0] * left[1] + right[1])

    _, hs = lax.associative_scan(combine, (a, b), axis=1)
    y = (hs * jax.nn.gelu(yb.astype(jnp.float32))).astype(h.dtype) @ w_out
    return y, xpad[:, -(CONV_W - 1):], hs[:, -1].astype(h0.dtype)


def dil_project(h, w_in, q_norm, k_norm):
    B, T, _ = h.shape
    proj = (h @ w_in).reshape(B, T, 3, DIL_GROUPS, DIL_HEADS, HEAD_DIM)
    q = rms_norm(proj[:, :, 0], q_norm) * (HEAD_DIM ** -0.5)
    kv = jnp.stack([rms_norm(proj[:, :, 1], k_norm), proj[:, :, 2]], axis=3)
    return q, kv


def dilated_attention(q, srcs, src_pos0s, q_pos0):
    B, Tq = q.shape[:2]
    qb = math.gcd(Tq, DIL_QB)
    padded = [jnp.pad(s, ((0, 0), (w, 0), (0, 0), (0, 0), (0, 0))) for s, (w, _) in zip(srcs, DIL_PAIRS)]

    def one_block(start):
        qblk = lax.dynamic_slice_in_dim(q, start, qb, axis=1)
        t = q_pos0 + start + jnp.arange(qb)
        outs, log_den = [], []
        for g, (w, r) in enumerate(DIL_PAIRS):
            kpos = t[:, None] - r * jnp.arange(w // r + 1)[None, :]
            kv = jnp.take(padded[g], kpos - src_pos0s[g] + w, axis=1, mode='clip')
            s = jnp.einsum('bqhd,bqnhd->bhqn', qblk[:, :, g], kv[:, :, :, 0]).astype(jnp.float32)
            s = jnp.where((kpos >= 0)[None, None], s, NEG)
            m = jnp.max(s, axis=-1, keepdims=True)
            e = jnp.exp(s - m)
            den = jnp.sum(e, axis=-1, keepdims=True)
            outs.append(jnp.einsum('bhqn,bqnhd->bqhd', e / den, kv[:, :, :, 1]))
            log_den.append(m + jnp.log(den))
        alpha = jax.nn.softmax(jnp.stack(log_den, 0), axis=0)
        alpha = jnp.transpose(alpha[..., 0], (0, 1, 3, 2))[..., None]
        return jnp.sum(alpha * jnp.stack(outs, 0), axis=0)

    out = lax.map(one_block, jnp.arange(Tq // qb) * qb)
    return jnp.moveaxis(out, 0, 1).reshape(B, Tq, DIL_HEADS * HEAD_DIM).astype(q.dtype)


def hier_moe(h, rg_w, rg_b, re_w, re_b, w_in, w_out):
    B, T, D = h.shape
    lg = (h @ rg_w + rg_b).astype(jnp.float32)
    g_top = jnp.argmax(lg, axis=-1)
    g_gate = jnp.take_along_axis(jax.nn.softmax(lg, axis=-1), g_top[..., None], axis=-1)
    le = (h @ re_w + re_b).astype(jnp.float32).reshape(B, T, N_EXPERT_GROUPS, EXPERTS_PER_GROUP)
    le = jnp.take_along_axis(le, g_top[..., None, None], axis=2)[:, :, 0]
    top_v, top_i = lax.top_k(le, EXPERT_TOP)
    gate = g_gate * jax.nn.softmax(top_v, axis=-1)
    eid = g_top[..., None] * EXPERTS_PER_GROUP + top_i
    comb = jnp.sum(jax.nn.one_hot(eid, N_EXPERTS, dtype=jnp.float32) * gate[..., None], axis=-2)
    y = jnp.zeros((B, T, D), jnp.float32)
    for e in range(N_EXPERTS):
        gu = h @ w_in[e]
        act = jax.nn.silu(gu[..., :D_EXPERT]) * gu[..., D_EXPERT:]
        y = y + comb[..., e:e + 1] * (act @ w_out[e])
    return y.astype(h.dtype)


def setup_inputs(seed: int = 0) -> dict:
    key = jax.random.key(seed)
    keys = jax.random.split(key, 64)
    counter = [0]

    def nxt():
        k = keys[counter[0]]
        counter[0] += 1
        return k

    def nrm(shape, scale):
        return jax.random.normal(nxt(), shape, jnp.float32) * scale

    n_nsa, n_lru, n_dil = (DEPTH + 2) // 3, (DEPTH + 1) // 3, DEPTH // 3
    n_pages = PAST_LEN // PAGE_SIZE
    n_pool = (DEC_BATCH * n_pages * 5) // 4
    page_table = jax.random.permutation(nxt(), n_pool)[:DEC_BATCH * n_pages].reshape(DEC_BATCH, n_pages).astype(jnp.int32)
    nsa_cols = NSA_HEADS * HEAD_DIM + 3 * NSA_HEADS + 6 * NSA_KV_HEADS * HEAD_DIM
    a0 = jax.random.uniform(nxt(), (n_lru, D_RNN), jnp.float32, minval=0.9, maxval=0.999)
    return {
        'x_prompt': nrm((BATCH, SEQ, D_MODEL), 1.0),
        'x_sample': nrm((DEC_BATCH, DEC_SEQ, D_MODEL), 1.0),
        'cache_nsa_kv': nrm((n_nsa, n_pool, PAGE_SIZE, 4, NSA_KV_HEADS, HEAD_DIM), 1.0),
        'cache_nsa_win': nrm((n_nsa, DEC_BATCH, min(NSA_WINDOW, PAST_LEN), 2, NSA_KV_HEADS, HEAD_DIM), 1.0),
        'state_lru_conv': nrm((n_lru, DEC_BATCH, CONV_W - 1, D_RNN), 1.0),
        'state_lru_h': nrm((n_lru, DEC_BATCH, D_RNN), 0.5),
        'cache_dil_win0': nrm((n_dil, DEC_BATCH, min(DIL_PAIRS[0][0], PAST_LEN), 2, DIL_HEADS, HEAD_DIM), 1.0),
        'cache_dil_win1': nrm((n_dil, DEC_BATCH, min(DIL_PAIRS[1][0], PAST_LEN), 2, DIL_HEADS, HEAD_DIM), 1.0),
        'cache_dil_win2': nrm((n_dil, DEC_BATCH, min(DIL_PAIRS[2][0], PAST_LEN), 2, DIL_HEADS, HEAD_DIM), 1.0),
        'page_table': page_table,
        'norm_mix': 1.0 + nrm((DEPTH, D_MODEL), 0.02),
        'norm_ffn': 1.0 + nrm((DEPTH, D_MODEL), 0.02),
        'nsa_w_in': nrm((n_nsa, D_MODEL, nsa_cols), D_MODEL ** -0.5),
        'nsa_q_norm': 1.0 + nrm((n_nsa, HEAD_DIM), 0.02),
        'nsa_k_norm': 1.0 + nrm((n_nsa, 3, HEAD_DIM), 0.02),
        'nsa_cmp_w1': nrm((n_nsa, 2, CMP_BLOCK, HEAD_DIM, PHI_HIDDEN), (CMP_BLOCK * HEAD_DIM) ** -0.5),
        'nsa_cmp_b1': nrm((n_nsa, 2, PHI_HIDDEN), 0.02),
        'nsa_cmp_w2': nrm((n_nsa, 2, PHI_HIDDEN, HEAD_DIM), PHI_HIDDEN ** -0.5),
        'nsa_cmp_b2': nrm((n_nsa, 2, HEAD_DIM), 0.02),
        'nsa_w_out': nrm((n_nsa, NSA_HEADS * HEAD_DIM, D_MODEL), (NSA_HEADS * HEAD_DIM) ** -0.5),
        'lru_w_in': nrm((n_lru, D_MODEL, 2 * D_RNN), D_MODEL ** -0.5),
        'lru_conv_w': nrm((n_lru, CONV_W, D_RNN), CONV_W ** -0.5),
        'lru_conv_b': nrm((n_lru, D_RNN), 0.02),
        'lru_gate_a_w': nrm((n_lru, LRU_BLOCKS, LRU_BLOCK_W, LRU_BLOCK_W), LRU_BLOCK_W ** -0.5),
        'lru_gate_a_b': nrm((n_lru, D_RNN), 0.02),
        'lru_gate_x_w': nrm((n_lru, LRU_BLOCKS, LRU_BLOCK_W, LRU_BLOCK_W), LRU_BLOCK_W ** -0.5),
        'lru_gate_x_b': nrm((n_lru, D_RNN), 0.02),
        'lru_lambda': jnp.log(a0) - jnp.log1p(-a0),
        'lru_w_out': nrm((n_lru, D_RNN, D_MODEL), D_RNN ** -0.5),
        'dil_w_in': nrm((n_dil, D_MODEL, 3 * DIL_GROUPS * DIL_HEADS * HEAD_DIM), D_MODEL ** -0.5),
        'dil_q_norm': 1.0 + nrm((n_dil, HEAD_DIM), 0.02),
        'dil_k_norm': 1.0 + nrm((n_dil, HEAD_DIM), 0.02),
        'dil_w_out': nrm((n_dil, DIL_HEADS * HEAD_DIM, D_MODEL), (DIL_HEADS * HEAD_DIM) ** -0.5),
        'moe_router_group_w': nrm((DEPTH, D_MODEL, N_EXPERT_GROUPS), D_MODEL ** -0.5),
        'moe_router_group_b': nrm((DEPTH, N_EXPERT_GROUPS), 0.01),
        'moe_router_expert_w': nrm((DEPTH, D_MODEL, N_EXPERTS), D_MODEL ** -0.5),
        'moe_router_expert_b': nrm((DEPTH, N_EXPERTS), 0.01),
        'moe_w_in': nrm((DEPTH, N_EXPERTS, D_MODEL, 2 * D_EXPERT), D_MODEL ** -0.5),
        'moe_w_out': nrm((DEPTH, N_EXPERTS, D_EXPERT, D_MODEL), D_EXPERT ** -0.5),
    }


def reference(x_prompt, x_sample, cache_nsa_kv, cache_nsa_win, state_lru_conv, state_lru_h,
              cache_dil_win0, cache_dil_win1, cache_dil_win2, page_table,
              norm_mix, norm_ffn,
              nsa_w_in, nsa_q_norm, nsa_k_norm, nsa_cmp_w1, nsa_cmp_b1, nsa_cmp_w2, nsa_cmp_b2, nsa_w_out,
              lru_w_in, lru_conv_w, lru_conv_b, lru_gate_a_w, lru_gate_a_b, lru_gate_x_w, lru_gate_x_b,
              lru_lambda, lru_w_out,
              dil_w_in, dil_q_norm, dil_k_norm, dil_w_out,
              moe_router_group_w, moe_router_group_b, moe_router_expert_w, moe_router_expert_b,
              moe_w_in, moe_w_out):
    past_len = page_table.shape[1] * PAGE_SIZE
    n_dec, dec_seq = x_sample.shape[0], x_sample.shape[1]
    n_pr, seq = x_prompt.shape[0], x_prompt.shape[1]
    nsa_wbuf = cache_nsa_win.shape[2]
    dil_caches = (cache_dil_win0, cache_dil_win1, cache_dil_win2)
    xp, xs = x_prompt, x_sample
    kv_p, kv_s, nw_p, nw_s = [], [], [], []
    cv_p, cv_s, hh_p, hh_s = [], [], [], []
    dw_p, dw_s = ([], [], []), ([], [], [])
    for layer in range(DEPTH):
        j = layer // N_MIXERS
        hp = rms_norm(xp, norm_mix[layer])
        hs = rms_norm(xs, norm_mix[layer])
        if layer % N_MIXERS == 0:
            phi = (nsa_k_norm[j][0], nsa_cmp_w1[j], nsa_cmp_b1[j], nsa_cmp_w2[j], nsa_cmp_b2[j])
            qp, gp, rows_p, win_p = nsa_project(hp, nsa_w_in[j], nsa_q_norm[j], nsa_k_norm[j])
            qs, gs, rows_s, win_s = nsa_project(hs, nsa_w_in[j], nsa_q_norm[j], nsa_k_norm[j])
            past_rows = cache_nsa_kv[j][page_table].reshape(n_dec, past_len, 4, NSA_KV_HEADS, HEAD_DIM)
            win_all = jnp.concatenate([cache_nsa_win[j], win_s], axis=1)
            op = nsa_attention(qp, gp, rows_p, win_p, 0, 0, *phi)
            os_ = nsa_attention(qs, gs, jnp.concatenate([past_rows, rows_s], axis=1), win_all,
                                past_len, past_len - nsa_wbuf, *phi)
            mp, ms = op @ nsa_w_out[j], os_ @ nsa_w_out[j]
            kv_p.append(rows_p)
            kv_s.append(rows_s)
            nw_p.append(win_p[:, seq - min(NSA_WINDOW, seq):])
            nw_s.append(win_all[:, dec_seq:])
        elif layer % N_MIXERS == 1:
            lp = (lru_w_in[j], lru_conv_w[j], lru_conv_b[j], lru_gate_a_w[j], lru_gate_a_b[j],
                  lru_gate_x_w[j], lru_gate_x_b[j], lru_lambda[j], lru_w_out[j])
            mp, c_p, h_p = rglru_mixer(hp, jnp.zeros((n_pr, CONV_W - 1, D_RNN), hp.dtype),
                                       jnp.zeros((n_pr, D_RNN), hp.dtype), *lp)
            ms, c_s, h_s = rglru_mixer(hs, state_lru_conv[j], state_lru_h[j], *lp)
            cv_p.append(c_p)
            cv_s.append(c_s)
            hh_p.append(h_p)
            hh_s.append(h_s)
        else:
            qp, kvp = dil_project(hp, dil_w_in[j], dil_q_norm[j], dil_k_norm[j])
            qs, kvs = dil_project(hs, dil_w_in[j], dil_q_norm[j], dil_k_norm[j])
            srcs_s = [jnp.concatenate([dil_caches[g][j], kvs[:, :, g]], axis=1) for g in range(DIL_GROUPS)]
            op = dilated_attention(qp, [kvp[:, :, g] for g in range(DIL_GROUPS)], (0, 0, 0), 0)
            os_ = dilated_attention(qs, srcs_s, tuple(past_len - c.shape[2] for c in dil_caches), past_len)
            mp, ms = op @ dil_w_out[j], os_ @ dil_w_out[j]
            for g, (w, _) in enumerate(DIL_PAIRS):
                dw_p[g].append(kvp[:, seq - min(w, seq):, g])
                dw_s[g].append(srcs_s[g][:, dec_seq:])
        xp = xp + mp
        xs = xs + ms
        moe = (moe_router_group_w[layer], moe_router_group_b[layer], moe_router_expert_w[layer],
               moe_router_expert_b[layer], moe_w_in[layer], moe_w_out[layer])
        xp = xp + hier_moe(rms_norm(xp, norm_ffn[layer]), *moe)
        xs = xs + hier_moe(rms_norm(xs, norm_ffn[layer]), *moe)
    return (xp, xs,
            jnp.stack(kv_p), jnp.stack(kv_s), jnp.stack(nw_p), jnp.stack(nw_s),
            jnp.stack(cv_p), jnp.stack(cv_s), jnp.stack(hh_p), jnp.stack(hh_s),
            jnp.stack(dw_p[0]), jnp.stack(dw_s[0]), jnp.stack(dw_p[1]), jnp.stack(dw_s[1]),
            jnp.stack(dw_p[2]), jnp.stack(dw_s[2]))
```

```python
import functools
import math

import numpy as np
import jax
import jax.numpy as jnp
from jax import lax
from jax.experimental import pallas as pl
from jax.experimental.pallas import tpu as pltpu

F32 = jnp.float32
BF16 = jnp.bfloat16

D_MODEL = 1024
N_MIXERS = 3
EPS = 1e-6
NEG = -1e30
BIG = 1e9
TINY = 1e-30
HEAD_DIM = 64

NSA_HEADS = 16
NSA_KV_HEADS = 4
NSA_GROUP = NSA_HEADS // NSA_KV_HEADS
CMP_BLOCK = 32
CMP_STRIDE = 16
SEL_BLOCK = 64
SEL_TOP = 16
NSA_WINDOW = 512
PHI_HIDDEN = 128
NSA_QB = 32

D_RNN = D_MODEL
CONV_W = 4
LRU_BLOCKS = 4
LRU_BLOCK_W = D_RNN // LRU_BLOCKS
LRU_C = 8.0

DIL_PAIRS = ((128, 1), (512, 4), (2048, 16))
DIL_GROUPS = 3
DIL_HEADS = 8
DIL_QB = 32

N_EXPERT_GROUPS = 4
EXPERTS_PER_GROUP = 4
N_EXPERTS = N_EXPERT_GROUPS * EXPERTS_PER_GROUP
EXPERT_TOP = 2
D_EXPERT = 512

PAGE_SIZE = 128

LANES = 128
SUBLANES = 8
VMEM_LIMIT = 56 * 1024 * 1024

TOKEN_TILE = 256
MOE_TILE = 640
LRU_CHUNK = 256


def _cparams(sem):
    return pltpu.CompilerParams(dimension_semantics=sem, vmem_limit_bytes=VMEM_LIMIT)


def _rms(x, g):
    ms = jnp.mean(x * x, axis=-1, keepdims=True)
    return x * lax.rsqrt(ms + EPS) * g


def _norm_proj_kernel(x_ref, g_ref, w_ref, o_ref):
    xn = _rms(x_ref[...], g_ref[...]).astype(BF16)
    o_ref[...] = jnp.dot(xn, w_ref[...], preferred_element_type=F32)


def norm_proj(x, g, w):
    m, d = x.shape
    n = w.shape[1]
    return pl.pallas_call(
        _norm_proj_kernel,
        grid=(m // TOKEN_TILE,),
        in_specs=[pl.BlockSpec((TOKEN_TILE, d), lambda i: (i, 0)),
                  pl.BlockSpec((1, d), lambda i: (0, 0)),
                  pl.BlockSpec((d, n), lambda i: (0, 0))],
        out_specs=pl.BlockSpec((TOKEN_TILE, n), lambda i: (i, 0)),
        out_shape=jax.ShapeDtypeStruct((m, n), F32),
        compiler_params=_cparams(("parallel",)),
        name="norm_proj",
    )(x, g.reshape(1, d), w)


def _matmul_res_kernel(a_ref, w_ref, r_ref, o_ref):
    o_ref[...] = r_ref[...] + jnp.dot(a_ref[...].astype(BF16), w_ref[...], preferred_element_type=F32)


def matmul_res(a, w, res):
    m, k = a.shape
    n = w.shape[1]
    return pl.pallas_call(
        _matmul_res_kernel,
        grid=(m // TOKEN_TILE,),
        in_specs=[pl.BlockSpec((TOKEN_TILE, k), lambda i: (i, 0)),
                  pl.BlockSpec((k, n), lambda i: (0, 0)),
                  pl.BlockSpec((TOKEN_TILE, n), lambda i: (i, 0))],
        out_specs=pl.BlockSpec((TOKEN_TILE, n), lambda i: (i, 0)),
        out_shape=jax.ShapeDtypeStruct((m, n), F32),
        compiler_params=_cparams(("parallel",)),
        name="matmul_res",
    )(a, w, res)


ROUTER_LANES = LANES


def _router_kernel(x_ref, g_ref, w_ref, b_ref, xn_ref, comb_ref):
    xn = _rms(x_ref[...], g_ref[...])
    xn_bf = xn.astype(BF16)
    xn_ref[...] = xn_bf
    z = jnp.dot(xn_bf, w_ref[...], preferred_element_type=F32) + b_ref[...]
    lane = lax.broadcasted_iota(jnp.int32, z.shape, 1)
    is_g = lane < N_EXPERT_GROUPS
    ninf = jnp.float32(-jnp.inf)
    far = jnp.int32(4 * LANES)
    zg = jnp.where(is_g, z, ninf)
    mg = jnp.max(zg, axis=-1, keepdims=True)
    g_top = jnp.min(jnp.where(is_g & (z == mg), lane, far), axis=-1, keepdims=True)
    den = jnp.sum(jnp.where(is_g, jnp.exp(zg - mg), 0.0), axis=-1, keepdims=True)
    g_gate = 1.0 / den
    lo = N_EXPERT_GROUPS + EXPERTS_PER_GROUP * g_top
    sel = (lane >= lo) & (lane < lo + EXPERTS_PER_GROUP)
    z1 = jnp.where(sel, z, ninf)
    v1 = jnp.max(z1, axis=-1, keepdims=True)
    i1 = jnp.min(jnp.where(sel & (z == v1), lane, far), axis=-1, keepdims=True)
    sel2 = sel & (lane != i1)
    z2 = jnp.where(sel2, z, ninf)
    v2 = jnp.max(z2, axis=-1, keepdims=True)
    i2 = jnp.min(jnp.where(sel2 & (z == v2), lane, far), axis=-1, keepdims=True)
    e2 = jnp.exp(v2 - v1)
    s = 1.0 + e2
    comb_ref[...] = jnp.where(lane == i1, g_gate * (1.0 / s),
                              jnp.where(lane == i2, g_gate * (e2 / s), 0.0))


def moe_router(x, g, w, b):
    m, d = x.shape
    return pl.pallas_call(
        _router_kernel,
        grid=(m // TOKEN_TILE,),
        in_specs=[pl.BlockSpec((TOKEN_TILE, d), lambda i: (i, 0)),
                  pl.BlockSpec((1, d), lambda i: (0, 0)),
                  pl.BlockSpec((d, ROUTER_LANES), lambda i: (0, 0)),
                  pl.BlockSpec((1, ROUTER_LANES), lambda i: (0, 0))],
        out_specs=[pl.BlockSpec((TOKEN_TILE, d), lambda i: (i, 0)),
                   pl.BlockSpec((TOKEN_TILE, ROUTER_LANES), lambda i: (i, 0))],
        out_shape=[jax.ShapeDtypeStruct((m, d), BF16),
                   jax.ShapeDtypeStruct((m, ROUTER_LANES), F32)],
        compiler_params=_cparams(("parallel",)),
        name="moe_router",
    )(x, g.reshape(1, d), w, b)


def _moe_dense_kernel(xn_ref, comb_ref, res_ref, win_ref, wout_ref, o_ref, acc_ref):
    e = pl.program_id(1)

    @pl.when(e == 0)
    def _():
        acc_ref[...] = jnp.zeros_like(acc_ref)

    gu = jnp.dot(xn_ref[...], win_ref[0].astype(BF16), preferred_element_type=F32)
    gate, up = gu[:, :D_EXPERT], gu[:, D_EXPERT:]
    act = (gate * jax.nn.sigmoid(gate)) * up
    y = jnp.dot(act.astype(BF16), wout_ref[0].astype(BF16), preferred_element_type=F32)
    comb = comb_ref[...]
    lane = lax.broadcasted_iota(jnp.int32, comb.shape, 1)
    ce = jnp.sum(jnp.where(lane == N_EXPERT_GROUPS + e, comb, 0.0), axis=-1, keepdims=True)
    acc_ref[...] += ce * y

    @pl.when(e == N_EXPERTS - 1)
    def _():
        o_ref[...] = res_ref[...] + acc_ref[...]


def moe_dense(xn, comb, res, w_in, w_out):
    m, d = xn.shape
    return pl.pallas_call(
        _moe_dense_kernel,
        grid=(m // MOE_TILE, N_EXPERTS),
        in_specs=[pl.BlockSpec((MOE_TILE, d), lambda i, e: (i, 0)),
                  pl.BlockSpec((MOE_TILE, ROUTER_LANES), lambda i, e: (i, 0)),
                  pl.BlockSpec((MOE_TILE, d), lambda i, e: (i, 0)),
                  pl.BlockSpec((1, d, 2 * D_EXPERT), lambda i, e: (e, 0, 0)),
                  pl.BlockSpec((1, D_EXPERT, d), lambda i, e: (e, 0, 0))],
        out_specs=pl.BlockSpec((MOE_TILE, d), lambda i, e: (i, 0)),
        out_shape=jax.ShapeDtypeStruct((m, d), F32),
        scratch_shapes=[pltpu.VMEM((MOE_TILE, d), F32)],
        compiler_params=_cparams(("parallel", "arbitrary")),
        name="moe_dense",
    )(xn, comb, res, w_in, w_out)


def hier_moe_res(x, g, rg_w, rg_b, re_w, re_b, w_in, w_out):
    d = x.shape[1]
    n_log = N_EXPERT_GROUPS + N_EXPERTS
    w = jnp.zeros((d, ROUTER_LANES), F32).at[:, :N_EXPERT_GROUPS].set(rg_w).at[:, N_EXPERT_GROUPS:n_log].set(re_w)
    b = jnp.zeros((1, ROUTER_LANES), F32).at[0, :N_EXPERT_GROUPS].set(rg_b).at[0, N_EXPERT_GROUPS:n_log].set(re_b)
    xn, comb = moe_router(x, g, w.astype(BF16), b)
    return moe_dense(xn, comb, x, w_in, w_out)


def _lru_gates(xc, ga_w_ref, ga_b, gx_w_ref, gx_b, lam):
    xc_bf = xc.astype(BF16)
    rs, is_ = [], []
    for n in range(LRU_BLOCKS):
        xg = xc_bf[:, n * LRU_BLOCK_W:(n + 1) * LRU_BLOCK_W]
        rs.append(jnp.dot(xg, ga_w_ref[n], preferred_element_type=F32))
        is_.append(jnp.dot(xg, gx_w_ref[n], preferred_element_type=F32))
    r = jax.nn.sigmoid(jnp.concatenate(rs, axis=-1) + ga_b)
    i = jax.nn.sigmoid(jnp.concatenate(is_, axis=-1) + gx_b)
    softplus_neg_lam = jnp.maximum(-lam, 0.0) + jnp.log1p(jnp.exp(-jnp.abs(lam)))
    log_a = -LRU_C * r * softplus_neg_lam
    a = jnp.exp(log_a)
    b = jnp.sqrt(-_expm1(2.0 * log_a)) * (i * xc)
    return a, b


def _expm1(z):
    u = jnp.exp(z)
    d = u - 1.0
    comp = d * z / jnp.log(u)
    return jnp.where(d == 0.0, z, jnp.where(z < -1.0, d, comp))


def _gelu(x):
    return jax.nn.gelu(x)


def _lru_prompt_kernel(xy_ref, cw_ref, cb_ref, ga_w_ref, ga_b_ref, gx_w_ref, gx_b_ref, lam_ref,
                       o_ref, hlast_ref, xext_ref, a_ref, b_ref, hs_ref, h_ref):
    c = pl.program_id(1)
    tc = LRU_CHUNK

    @pl.when(c == 0)
    def _():
        xext_ref[0:SUBLANES, :] = jnp.zeros((SUBLANES, D_RNN), F32)
        h_ref[...] = jnp.zeros_like(h_ref)

    xb = xy_ref[0, :, :D_RNN]
    xext_ref[SUBLANES:, :] = xb
    xc = cb_ref[...]
    for k in range(CONV_W):
        off = SUBLANES - (CONV_W - 1) + k
        xc = xc + xext_ref[off:off + tc, :] * cw_ref[k:k + 1, :]
    xext_ref[0:SUBLANES, :] = xb[tc - SUBLANES:, :]
    a, b = _lru_gates(xc, ga_w_ref, ga_b_ref[...], gx_w_ref, gx_b_ref[...], lam_ref[...])
    a_ref[...] = a
    b_ref[...] = b

    def step(t, h):
        h = a_ref[pl.ds(t, 1), :] * h + b_ref[pl.ds(t, 1), :]
        hs_ref[pl.ds(t, 1), :] = h
        return h

    h = lax.fori_loop(0, tc, step, h_ref[...], unroll=8)
    h_ref[...] = h
    hlast_ref[0] = h
    o_ref[0] = (hs_ref[...] * _gelu(xy_ref[0, :, D_RNN:])).astype(o_ref.dtype)


def lru_prompt(xy, cw, cb, ga_w, ga_b, gx_w, gx_b, lam):
    bsz, t, _ = xy.shape
    tc = LRU_CHUNK
    row = lambda v: v.reshape(1, D_RNN)
    full2 = lambda shp: pl.BlockSpec(shp, lambda b, c: (0, 0))
    full3 = lambda shp: pl.BlockSpec(shp, lambda b, c: (0, 0, 0))
    return pl.pallas_call(
        _lru_prompt_kernel,
        grid=(bsz, t // tc),
        in_specs=[pl.BlockSpec((1, tc, 2 * D_RNN), lambda b, c: (b, c, 0)),
                  full2((CONV_W, D_RNN)), full2((1, D_RNN)),
                  full3((LRU_BLOCKS, LRU_BLOCK_W, LRU_BLOCK_W)), full2((1, D_RNN)),
                  full3((LRU_BLOCKS, LRU_BLOCK_W, LRU_BLOCK_W)), full2((1, D_RNN)),
                  full2((1, D_RNN))],
        out_specs=[pl.BlockSpec((1, tc, D_RNN), lambda b, c: (b, c, 0)),
                   pl.BlockSpec((1, 1, D_RNN), lambda b, c: (b, 0, 0))],
        out_shape=[jax.ShapeDtypeStruct((bsz, t, D_RNN), BF16),
                   jax.ShapeDtypeStruct((bsz, 1, D_RNN), F32)],
        scratch_shapes=[pltpu.VMEM((tc + SUBLANES, D_RNN), F32),
                        pltpu.VMEM((tc, D_RNN), F32),
                        pltpu.VMEM((tc, D_RNN), F32),
                        pltpu.VMEM((tc, D_RNN), F32),
                        pltpu.VMEM((1, D_RNN), F32)],
        compiler_params=_cparams(("parallel", "arbitrary")),
        name="lru_prompt",
    )(xy, cw, row(cb), ga_w.astype(BF16), row(ga_b), gx_w.astype(BF16), row(gx_b), row(lam))


def _lru_step_kernel(xy_ref, cbuf_ref, h0_ref, cw_ref, cb_ref, ga_w_ref, ga_b_ref, gx_w_ref, gx_b_ref, lam_ref,
                     o_ref, h_ref):
    xb = xy_ref[:, :D_RNN]
    xc = cb_ref[...]
    for k in range(CONV_W - 1):
        xc = xc + cbuf_ref[k] * cw_ref[k:k + 1, :]
    xc = xc + xb * cw_ref[CONV_W - 1:CONV_W, :]
    a, b = _lru_gates(xc, ga_w_ref, ga_b_ref[...], gx_w_ref, gx_b_ref[...], lam_ref[...])
    h = a * h0_ref[...] + b
    h_ref[...] = h
    o_ref[...] = (h * _gelu(xy_ref[:, D_RNN:])).astype(o_ref.dtype)


def lru_step(xy, cbuf, h0, cw, cb, ga_w, ga_b, gx_w, gx_b, lam):
    bsz = xy.shape[0]
    row = lambda v: v.reshape(1, D_RNN)
    return pl.pallas_call(
        _lru_step_kernel,
        out_shape=[jax.ShapeDtypeStruct((bsz, D_RNN), BF16),
                   jax.ShapeDtypeStruct((bsz, D_RNN), F32)],
        compiler_params=pltpu.CompilerParams(vmem_limit_bytes=VMEM_LIMIT),
        name="lru_step",
    )(xy, cbuf, h0, cw, row(cb), ga_w.astype(BF16), row(ga_b), gx_w.astype(BF16), row(gx_b), row(lam))


def rms_norm(x, g):
    xf = x.astype(jnp.float32)
    y = xf * lax.rsqrt(jnp.mean(xf * xf, axis=-1, keepdims=True) + EPS)
    return (y * g.astype(jnp.float32)).astype(x.dtype)


def masked_softmax(s, mask):
    s = jnp.where(mask, s, NEG)
    e = jnp.exp(s - jnp.max(s, axis=-1, keepdims=True)) * mask
    return e / jnp.maximum(jnp.sum(e, axis=-1, keepdims=True), TINY)


def nsa_split(proj, q_norm, k_norm):
    B, T, _ = proj.shape
    qd = NSA_HEADS * HEAD_DIM
    q = rms_norm(proj[..., :qd].reshape(B, T, NSA_HEADS, HEAD_DIM), q_norm) * (HEAD_DIM ** -0.5)
    gates = jax.nn.sigmoid(proj[..., qd:qd + 3 * NSA_HEADS].astype(jnp.float32)).reshape(B, T, NSA_HEADS, 3)
    kv = proj[..., qd + 3 * NSA_HEADS:].reshape(B, T, 6, NSA_KV_HEADS, HEAD_DIM)
    rows = jnp.stack([kv[:, :, 0], kv[:, :, 1], rms_norm(kv[:, :, 2], k_norm[1]), kv[:, :, 3]], axis=2)
    win = jnp.stack([rms_norm(kv[:, :, 4], k_norm[2]), kv[:, :, 5]], axis=2)
    return q, gates, rows, win


def compress_rows(r, w1, b1, w2, b2):
    B, L, KV, Dh = r.shape
    n_half = -(-L // CMP_STRIDE)
    r = jnp.pad(r, ((0, 0), (0, n_half * CMP_STRIDE - L), (0, 0), (0, 0))).reshape(B, n_half, CMP_STRIDE, KV, Dh)
    lead = jnp.einsum('bnskd,sdh->bnkh', r[:, :-1], w1[:CMP_STRIDE])
    tail = jnp.einsum('bnskd,sdh->bnkh', r[:, 1:], w1[CMP_STRIDE:])
    hid = jax.nn.gelu(lead + tail + b1)
    return jnp.einsum('bnkh,hd->bnkd', hid, w2) + b2


def nsa_attention(q, gates, rows, win, q_pos0, w_pos0, k_cmp_norm, cmp_w1, cmp_b1, cmp_w2, cmp_b2):
    B, Tq, H, Dh = q.shape
    L = rows.shape[1]
    kc = rms_norm(compress_rows(rows[:, :, 0], cmp_w1[0], cmp_b1[0], cmp_w2[0], cmp_b2[0]), k_cmp_norm)
    vc = compress_rows(rows[:, :, 1], cmp_w1[1], cmp_b1[1], cmp_w2[1], cmp_b2[1])
    n_cmp = kc.shape[1]
    n_sel = -(-L // SEL_BLOCK)
    sel_rows = jnp.pad(rows[:, :, 2:], ((0, 0), (0, n_sel * SEL_BLOCK - L), (0, 0), (0, 0), (0, 0)))
    sel_rows = sel_rows.reshape(B, n_sel, SEL_BLOCK, 2, NSA_KV_HEADS, Dh).transpose(0, 4, 1, 2, 3, 5)
    n_top = min(SEL_TOP, n_sel)
    win_pad = jnp.pad(win, ((0, 0), (NSA_WINDOW - 1, 0), (0, 0), (0, 0), (0, 0)))
    qb = math.gcd(Tq, NSA_QB)
    w_len = NSA_WINDOW + qb - 1
    ratio = SEL_BLOCK // CMP_STRIDE
    span = CMP_BLOCK // CMP_STRIDE
    map_idx = np.arange(n_sel)[:, None] * ratio - (span - 1) + np.arange(ratio + span - 1)[None, :]
    map_w = np.convolve(np.ones(ratio), np.ones(span))[None, :] * ((map_idx >= 0) & (map_idx < n_cmp))
    map_idx = jnp.asarray(np.clip(map_idx, 0, n_cmp - 1), jnp.int32)
    map_w = jnp.asarray(map_w, jnp.float32)
    cmp_end = jnp.arange(n_cmp) * CMP_STRIDE + CMP_BLOCK - 1
    blk = jnp.arange(n_sel)
    qg = q.reshape(B, Tq, NSA_KV_HEADS, NSA_GROUP, Dh)
    gg = gates.reshape(B, Tq, NSA_KV_HEADS, NSA_GROUP, 3)
    bi = jnp.arange(B)[:, None, None, None]
    ki = jnp.arange(NSA_KV_HEADS)[None, :, None, None]

    def one_block(start):
        qblk = lax.dynamic_slice_in_dim(qg, start, qb, axis=1)
        gblk = lax.dynamic_slice_in_dim(gg, start, qb, axis=1)
        t = q_pos0 + start + jnp.arange(qb)
        s_c = jnp.einsum('bqkgd,bnkd->bkgqn', qblk, kc).astype(jnp.float32)
        p_c = masked_softmax(s_c, cmp_end[None, :] <= t[:, None])
        o_c = jnp.einsum('bkgqn,bnkd->bqkgd', p_c, vc)
        imp = jnp.sum(p_c, axis=2)
        p_blk = jnp.sum(jnp.take(imp, map_idx, axis=-1) * map_w, axis=-1)
        cur = t[:, None] // SEL_BLOCK
        forced = (blk[None, :] == 0) | (blk[None, :] == cur) | (blk[None, :] == cur - 1)
        score = jnp.where(blk[None, :] * SEL_BLOCK > t[:, None], NEG, jnp.where(forced, BIG, p_blk))
        _, sel = lax.top_k(score, n_top)
        kv_sel = sel_rows[bi, ki, sel].reshape(B, NSA_KV_HEADS, qb, n_top * SEL_BLOCK, 2, Dh)
        pos = (sel[..., None] * SEL_BLOCK + jnp.arange(SEL_BLOCK)).reshape(B, NSA_KV_HEADS, qb, n_top * SEL_BLOCK)
        s_s = jnp.einsum('bqkgd,bkqmd->bkgqm', qblk, kv_sel[..., 0, :]).astype(jnp.float32)
        p_s = masked_softmax(s_s, (pos <= t[:, None])[:, :, None])
        o_s = jnp.einsum('bkgqm,bkqmd->bqkgd', p_s, kv_sel[..., 1, :])
        kw = lax.dynamic_slice_in_dim(win_pad, q_pos0 + start - w_pos0, w_len, axis=1)
        kpos = q_pos0 + start - (NSA_WINDOW - 1) + jnp.arange(w_len)
        dist = t[:, None] - kpos[None, :]
        wmask = (kpos[None, :] >= 0) & (dist >= 0) & (dist < NSA_WINDOW)
        s_w = jnp.einsum('bqkgd,bmkd->bkgqm', qblk, kw[:, :, 0]).astype(jnp.float32)
        p_w = masked_softmax(s_w, wmask)
        o_w = jnp.einsum('bkgqm,bmkd->bqkgd', p_w, kw[:, :, 1])
        return gblk[..., 0:1] * o_c + gblk[..., 1:2] * o_s + gblk[..., 2:3] * o_w

    out = lax.map(one_block, jnp.arange(Tq // qb) * qb)
    return jnp.moveaxis(out, 0, 1).reshape(B, Tq, H * Dh).astype(q.dtype)


def dil_split(proj, q_norm, k_norm):
    B, T, _ = proj.shape
    proj = proj.reshape(B, T, 3, DIL_GROUPS, DIL_HEADS, HEAD_DIM)
    q = rms_norm(proj[:, :, 0], q_norm) * (HEAD_DIM ** -0.5)
    kv = jnp.stack([rms_norm(proj[:, :, 1], k_norm), proj[:, :, 2]], axis=3)
    return q, kv


def dilated_attention(q, srcs, src_pos0s, q_pos0):
    B, Tq = q.shape[:2]
    qb = math.gcd(Tq, DIL_QB)
    padded = [jnp.pad(s, ((0, 0), (w, 0), (0, 0), (0, 0), (0, 0))) for s, (w, _) in zip(srcs, DIL_PAIRS)]

    def one_block(start):
        qblk = lax.dynamic_slice_in_dim(q, start, qb, axis=1)
        t = q_pos0 + start + jnp.arange(qb)
        outs, log_den = [], []
        for g, (w, r) in enumerate(DIL_PAIRS):
            kpos = t[:, None] - r * jnp.arange(w // r + 1)[None, :]
            kv = jnp.take(padded[g], kpos - src_pos0s[g] + w, axis=1, mode='clip')
            s = jnp.einsum('bqhd,bqnhd->bhqn', qblk[:, :, g], kv[:, :, :, 0]).astype(jnp.float32)
            s = jnp.where((kpos >= 0)[None, None], s, NEG)
            m = jnp.max(s, axis=-1, keepdims=True)
            e = jnp.exp(s - m)
            den = jnp.sum(e, axis=-1, keepdims=True)
            outs.append(jnp.einsum('bhqn,bqnhd->bqhd', e / den, kv[:, :, :, 1]))
            log_den.append(m + jnp.log(den))
        alpha = jax.nn.softmax(jnp.stack(log_den, 0), axis=0)
        alpha = jnp.transpose(alpha[..., 0], (0, 1, 3, 2))[..., None]
        return jnp.sum(alpha * jnp.stack(outs, 0), axis=0)

    out = lax.map(one_block, jnp.arange(Tq // qb) * qb)
    return jnp.moveaxis(out, 0, 1).reshape(B, Tq, DIL_HEADS * HEAD_DIM).astype(q.dtype)


def kernel(x_prompt, x_sample, cache_nsa_kv, cache_nsa_win, state_lru_conv, state_lru_h, cache_dil_win0, cache_dil_win1, cache_dil_win2, page_table, norm_mix, norm_ffn, nsa_w_in, nsa_q_norm, nsa_k_norm, nsa_cmp_w1, nsa_cmp_b1, nsa_cmp_w2, nsa_cmp_b2, nsa_w_out, lru_w_in, lru_conv_w, lru_conv_b, lru_gate_a_w, lru_gate_a_b, lru_gate_x_w, lru_gate_x_b, lru_lambda, lru_w_out, dil_w_in, dil_q_norm, dil_k_norm, dil_w_out, moe_router_group_w, moe_router_group_b, moe_router_expert_w, moe_router_expert_b, moe_w_in, moe_w_out):
    depth = norm_mix.shape[0]
    n_pr, seq, d = x_prompt.shape
    n_dec, dec_seq, _ = x_sample.shape
    assert dec_seq == 1
    past_len = page_table.shape[1] * PAGE_SIZE
    nsa_wbuf = cache_nsa_win.shape[2]
    dil_caches = (cache_dil_win0, cache_dil_win1, cache_dil_win2)
    n_p = n_pr * seq
    n_s = n_dec * dec_seq
    m_all = -(-(n_p + n_s) // math.lcm(TOKEN_TILE, MOE_TILE)) * math.lcm(TOKEN_TILE, MOE_TILE)
    x = jnp.concatenate([x_prompt.reshape(n_p, d), x_sample.reshape(n_s, d),
                         jnp.zeros((m_all - n_p - n_s, d), F32)], axis=0)

    def split(y):
        return y[:n_p].reshape(n_pr, seq, -1), y[n_p:n_p + n_s].reshape(n_dec, dec_seq, -1)

    def join(yp, ys):
        k = yp.shape[-1]
        return jnp.concatenate([yp.reshape(n_p, k).astype(F32), ys.reshape(n_s, k).astype(F32),
                                jnp.zeros((m_all - n_p - n_s, k), F32)], axis=0)

    kv_p, kv_s, nw_p, nw_s = [], [], [], []
    cv_p, cv_s, hh_p, hh_s = [], [], [], []
    dw_p, dw_s = ([], [], []), ([], [], [])
    for layer in range(depth):
        j = layer // N_MIXERS
        if layer % N_MIXERS == 0:
            proj = norm_proj(x, norm_mix[layer], nsa_w_in[j].astype(BF16))
            pp, ps = split(proj)
            phi = (nsa_k_norm[j][0], nsa_cmp_w1[j], nsa_cmp_b1[j], nsa_cmp_w2[j], nsa_cmp_b2[j])
            qp, gp, rows_p, win_p = nsa_split(pp, nsa_q_norm[j], nsa_k_norm[j])
            qs, gs, rows_s, win_s = nsa_split(ps, nsa_q_norm[j], nsa_k_norm[j])
            past_rows = cache_nsa_kv[j][page_table].reshape(n_dec, past_len, 4, NSA_KV_HEADS, HEAD_DIM)
            win_all = jnp.concatenate([cache_nsa_win[j], win_s], axis=1)
            op = nsa_attention(qp, gp, rows_p, win_p, 0, 0, *phi)
            os_ = nsa_attention(qs, gs, jnp.concatenate([past_rows, rows_s], axis=1), win_all,
                                past_len, past_len - nsa_wbuf, *phi)
            x = matmul_res(join(op, os_), nsa_w_out[j].astype(BF16), x)
            kv_p.append(rows_p)
            kv_s.append(rows_s)
            nw_p.append(win_p[:, seq - min(NSA_WINDOW, seq):])
            nw_s.append(win_all[:, dec_seq:])
        elif layer % N_MIXERS == 1:
            xy = norm_proj(x, norm_mix[layer], lru_w_in[j].astype(BF16))
            xy_p, xy_s = split(xy)
            lp = (lru_conv_w[j], lru_conv_b[j], lru_gate_a_w[j], lru_gate_a_b[j],
                  lru_gate_x_w[j], lru_gate_x_b[j], lru_lambda[j])
            op, h_p = lru_prompt(xy_p, *lp)
            os_, h_s = lru_step(xy_s[:, 0], jnp.swapaxes(state_lru_conv[j], 0, 1), state_lru_h[j], *lp)
            x = matmul_res(join(op, os_), lru_w_out[j].astype(BF16), x)
            cv_p.append(xy_p[:, seq - (CONV_W - 1):, :D_RNN])
            cv_s.append(jnp.concatenate([state_lru_conv[j], xy_s[:, :, :D_RNN]], axis=1)[:, -(CONV_W - 1):])
            hh_p.append(h_p[:, 0])
            hh_s.append(h_s)
        else:
            proj = norm_proj(x, norm_mix[layer], dil_w_in[j].astype(BF16))
            pp, ps = split(proj)
            qp, kvp = dil_split(pp, dil_q_norm[j], dil_k_norm[j])
            qs, kvs = dil_split(ps, dil_q_norm[j], dil_k_norm[j])
            srcs_s = [jnp.concatenate([dil_caches[g][j], kvs[:, :, g]], axis=1) for g in range(DIL_GROUPS)]
            op = dilated_attention(qp, [kvp[:, :, g] for g in range(DIL_GROUPS)], (0, 0, 0), 0)
            os_ = dilated_attention(qs, srcs_s, tuple(past_len - c.shape[2] for c in dil_caches), past_len)
            x = matmul_res(join(op, os_), dil_w_out[j].astype(BF16), x)
            for g, (w, _) in enumerate(DIL_PAIRS):
                dw_p[g].append(kvp[:, seq - min(w, seq):, g])
                dw_s[g].append(srcs_s[g][:, dec_seq:])
        x = hier_moe_res(x, norm_ffn[layer], moe_router_group_w[layer], moe_router_group_b[layer],
                         moe_router_expert_w[layer], moe_router_expert_b[layer], moe_w_in[layer], moe_w_out[layer])
    xp, xs = split(x)
    return (xp, xs,
            jnp.stack(kv_p), jnp.stack(kv_s), jnp.stack(nw_p), jnp.stack(nw_s),
            jnp.stack(cv_p), jnp.stack(cv_s), jnp.stack(hh_p), jnp.stack(hh_s),
            jnp.stack(dw_p[0]), jnp.stack(dw_s[0]), jnp.stack(dw_p[1]), jnp.stack(dw_s[1]),
            jnp.stack(dw_p[2]), jnp.stack(dw_s[2]))
```

```python
import functools

import numpy as np
import jax
import jax.numpy as jnp
from jax import lax
from jax.experimental import pallas as pl
from jax.experimental.pallas import tpu as pltpu

F32 = jnp.float32
BF16 = jnp.bfloat16

D_MODEL = 1024
N_MIXERS = 3
EPS = 1e-6
NEG = -1e30
BIG = 1e9
TINY = 1e-30
HEAD_DIM = 64

NSA_HEADS = 16
NSA_KV_HEADS = 4
NSA_GROUP = NSA_HEADS // NSA_KV_HEADS
CMP_BLOCK = 32
CMP_STRIDE = 16
SEL_BLOCK = 64
SEL_TOP = 16
NSA_WINDOW = 512
PHI_HIDDEN = 128

D_RNN = D_MODEL
CONV_W = 4
LRU_BLOCKS = 4
LRU_BLOCK_W = D_RNN // LRU_BLOCKS
LRU_C = 8.0

DIL_PAIRS = ((128, 1), (512, 4), (2048, 16))
DIL_GROUPS = 3
DIL_HEADS = 8

N_EXPERT_GROUPS = 4
EXPERTS_PER_GROUP = 4
N_EXPERTS = N_EXPERT_GROUPS * EXPERTS_PER_GROUP
D_EXPERT = 512

PAGE_SIZE = 128

LANES = 128
SUBLANES = 8
VMEM_LIMIT = 56 * 1024 * 1024

TOKEN_TILE = 256
MOE_TILE = 512
LRU_CHUNK = 256


def _cparams(sem):
    return pltpu.CompilerParams(dimension_semantics=sem, vmem_limit_bytes=VMEM_LIMIT)


def _rms(x, g):
    ms = jnp.mean(x * x, axis=-1, keepdims=True)
    return x * lax.rsqrt(ms + EPS) * g


def _split2(x):
    hi = x.astype(BF16)
    return hi, (x - hi.astype(F32)).astype(BF16)


def _split2_masked(x):
    hi = lax.bitcast_convert_type(lax.bitcast_convert_type(x, jnp.uint32) & jnp.uint32(0xFFFF0000), F32)
    return hi.astype(BF16), (x - hi).astype(BF16)


def _split3(x):
    h1 = x.astype(BF16)
    r1 = x - h1.astype(F32)
    h2 = r1.astype(BF16)
    h3 = (r1 - h2.astype(F32)).astype(BF16)
    return h1, h2, h3


def _dot_nn(a, b):
    return jnp.dot(a, b, preferred_element_type=F32)


def _dot_nt(a, b):
    return lax.dot_general(a, b, (((1,), (1,)), ((), ())), preferred_element_type=F32)


def _dot3_parts(a, b, dot):
    return dot(a[0], b[0]) + (dot(a[0], b[1]) + dot(a[1], b[0]))


def _mm(a, w, hp):
    if hp:
        return _dot3_parts(_split2(a.astype(F32)), _split2(w.astype(F32)), _dot_nn)
    return _dot_nn(a.astype(BF16), w.astype(BF16))


def _mm_nt(a, b, hp):
    if hp:
        return _dot3_parts(_split2(a.astype(F32)), _split2(b.astype(F32)), _dot_nt)
    return _dot_nt(a.astype(BF16), b.astype(BF16))


def _wdtype(hp):
    return F32 if hp else BF16


def _tile(m):
    return min(TOKEN_TILE, m)


def _norm_proj_kernel(x_ref, g_ref, w_ref, o_ref, *, hp):
    o_ref[...] = _mm(_rms(x_ref[...], g_ref[...]), w_ref[...], hp)


def norm_proj(x, g, w, hp=False):
    m, d = x.shape
    n = w.shape[1]
    tm = _tile(m)
    return pl.pallas_call(
        functools.partial(_norm_proj_kernel, hp=hp),
        grid=(m // tm,),
        in_specs=[pl.BlockSpec((tm, d), lambda i: (i, 0)),
                  pl.BlockSpec((1, d), lambda i: (0, 0)),
                  pl.BlockSpec((d, n), lambda i: (0, 0))],
        out_specs=pl.BlockSpec((tm, n), lambda i: (i, 0)),
        out_shape=jax.ShapeDtypeStruct((m, n), F32),
        compiler_params=_cparams(("parallel",)),
        name="norm_proj",
    )(x, g.reshape(1, d), w.astype(_wdtype(hp)))


def _matmul_res_kernel(a_ref, w_ref, r_ref, o_ref, *, hp):
    o_ref[...] = r_ref[...] + _mm(a_ref[...], w_ref[...], hp)


def matmul_res(a, w, res, hp=False):
    m, k = a.shape
    n = w.shape[1]
    tm = _tile(m)
    return pl.pallas_call(
        functools.partial(_matmul_res_kernel, hp=hp),
        grid=(m // tm,),
        in_specs=[pl.BlockSpec((tm, k), lambda i: (i, 0)),
                  pl.BlockSpec((k, n), lambda i: (0, 0)),
                  pl.BlockSpec((tm, n), lambda i: (i, 0))],
        out_specs=pl.BlockSpec((tm, n), lambda i: (i, 0)),
        out_shape=jax.ShapeDtypeStruct((m, n), F32),
        compiler_params=_cparams(("parallel",)),
        name="matmul_res",
    )(a, w.astype(_wdtype(hp)), res)


ROUTER_LANES = LANES


def _router_kernel(x_ref, g_ref, w_ref, b_ref, xn_ref, comb_ref):
    xn = _rms(x_ref[...], g_ref[...])
    xn_ref[...] = xn.astype(xn_ref.dtype)
    z = _mm(xn, w_ref[...], True) + b_ref[...]
    lane = lax.broadcasted_iota(jnp.int32, z.shape, 1)
    is_g = lane < N_EXPERT_GROUPS
    ninf = jnp.float32(-jnp.inf)
    far = jnp.int32(4 * LANES)
    zg = jnp.where(is_g, z, ninf)
    mg = jnp.max(zg, axis=-1, keepdims=True)
    g_top = jnp.min(jnp.where(is_g & (z == mg), lane, far), axis=-1, keepdims=True)
    den = jnp.sum(jnp.where(is_g, jnp.exp(zg - mg), 0.0), axis=-1, keepdims=True)
    g_gate = 1.0 / den
    lo = N_EXPERT_GROUPS + EXPERTS_PER_GROUP * g_top
    sel = (lane >= lo) & (lane < lo + EXPERTS_PER_GROUP)
    z1 = jnp.where(sel, z, ninf)
    v1 = jnp.max(z1, axis=-1, keepdims=True)
    i1 = jnp.min(jnp.where(sel & (z == v1), lane, far), axis=-1, keepdims=True)
    sel2 = sel & (lane != i1)
    z2 = jnp.where(sel2, z, ninf)
    v2 = jnp.max(z2, axis=-1, keepdims=True)
    i2 = jnp.min(jnp.where(sel2 & (z == v2), lane, far), axis=-1, keepdims=True)
    e2 = jnp.exp(v2 - v1)
    s = 1.0 + e2
    comb_ref[...] = jnp.where(lane == i1, g_gate * (1.0 / s),
                              jnp.where(lane == i2, g_gate * (e2 / s), 0.0))


def moe_router(x, g, w, b, hp):
    m, d = x.shape
    tm = _tile(m)
    return pl.pallas_call(
        _router_kernel,
        grid=(m // tm,),
        in_specs=[pl.BlockSpec((tm, d), lambda i: (i, 0)),
                  pl.BlockSpec((1, d), lambda i: (0, 0)),
                  pl.BlockSpec((d, ROUTER_LANES), lambda i: (0, 0)),
                  pl.BlockSpec((1, ROUTER_LANES), lambda i: (0, 0))],
        out_specs=[pl.BlockSpec((tm, d), lambda i: (i, 0)),
                   pl.BlockSpec((tm, ROUTER_LANES), lambda i: (i, 0))],
        out_shape=[jax.ShapeDtypeStruct((m, d), _wdtype(hp)),
                   jax.ShapeDtypeStruct((m, ROUTER_LANES), F32)],
        compiler_params=_cparams(("parallel",)),
        name="moe_router",
    )(x, g.reshape(1, d), w, b)


def _moe_dense_kernel(xn_ref, comb_ref, res_ref, win_ref, wout_ref, o_ref, acc_ref, *, hp):
    e = pl.program_id(1)

    @pl.when(e == 0)
    def _():
        acc_ref[...] = jnp.zeros_like(acc_ref)

    gu = _mm(xn_ref[...], win_ref[0], hp)
    gate, up = gu[:, :D_EXPERT], gu[:, D_EXPERT:]
    act = (gate * jax.nn.sigmoid(gate)) * up
    y = _mm(act, wout_ref[0], hp)
    comb = comb_ref[...]
    lane = lax.broadcasted_iota(jnp.int32, comb.shape, 1)
    ce = jnp.sum(jnp.where(lane == N_EXPERT_GROUPS + e, comb, 0.0), axis=-1, keepdims=True)
    acc_ref[...] += ce * y

    @pl.when(e == N_EXPERTS - 1)
    def _():
        o_ref[...] = res_ref[...] + acc_ref[...]


def moe_dense(xn, comb, res, w_in, w_out, hp):
    m, d = xn.shape
    tm = min(MOE_TILE, m)
    return pl.pallas_call(
        functools.partial(_moe_dense_kernel, hp=hp),
        grid=(m // tm, N_EXPERTS),
        in_specs=[pl.BlockSpec((tm, d), lambda i, e: (i, 0)),
                  pl.BlockSpec((tm, ROUTER_LANES), lambda i, e: (i, 0)),
                  pl.BlockSpec((tm, d), lambda i, e: (i, 0)),
                  pl.BlockSpec((1, d, 2 * D_EXPERT), lambda i, e: (e, 0, 0)),
                  pl.BlockSpec((1, D_EXPERT, d), lambda i, e: (e, 0, 0))],
        out_specs=pl.BlockSpec((tm, d), lambda i, e: (i, 0)),
        out_shape=jax.ShapeDtypeStruct((m, d), F32),
        scratch_shapes=[pltpu.VMEM((tm, d), F32)],
        compiler_params=_cparams(("parallel", "arbitrary")),
        name="moe_dense",
    )(xn, comb, res, w_in, w_out)


def hier_moe_res(x, g, rg_w, rg_b, re_w, re_b, w_in, w_out, hp=False):
    d = x.shape[1]
    n_log = N_EXPERT_GROUPS + N_EXPERTS
    w = jnp.zeros((d, ROUTER_LANES), F32).at[:, :N_EXPERT_GROUPS].set(rg_w).at[:, N_EXPERT_GROUPS:n_log].set(re_w)
    b = jnp.zeros((1, ROUTER_LANES), F32).at[0, :N_EXPERT_GROUPS].set(rg_b).at[0, N_EXPERT_GROUPS:n_log].set(re_b)
    xn, comb = moe_router(x, g, w, b, hp)
    return moe_dense(xn, comb, x, w_in, w_out, hp)


def _expm1(z):
    u = jnp.exp(z)
    d = u - 1.0
    comp = d * z / jnp.log(u)
    return jnp.where(d == 0.0, z, jnp.where(z < -1.0, d, comp))


def _gelu(x):
    return jax.nn.gelu(x)


def _lru_gates(xc, ga_w_ref, ga_b, gx_w_ref, gx_b, lam, hp):
    rs, is_ = [], []
    for n in range(LRU_BLOCKS):
        xg = xc[:, n * LRU_BLOCK_W:(n + 1) * LRU_BLOCK_W]
        rs.append(_mm(xg, ga_w_ref[n], hp))
        is_.append(_mm(xg, gx_w_ref[n], hp))
    r = jax.nn.sigmoid(jnp.concatenate(rs, axis=-1) + ga_b)
    i = jax.nn.sigmoid(jnp.concatenate(is_, axis=-1) + gx_b)
    softplus_neg_lam = jnp.maximum(-lam, 0.0) + jnp.log1p(jnp.exp(-jnp.abs(lam)))
    log_a = -LRU_C * r * softplus_neg_lam
    a = jnp.exp(log_a)
    b = jnp.sqrt(-_expm1(2.0 * log_a)) * (i * xc)
    return a, b


def _lru_prompt_kernel(xy_ref, cw_ref, cb_ref, ga_w_ref, ga_b_ref, gx_w_ref, gx_b_ref, lam_ref,
                       o_ref, hlast_ref, xext_ref, a_ref, b_ref, hs_ref, h_ref):
    c = pl.program_id(1)
    tc = LRU_CHUNK

    @pl.when(c == 0)
    def _():
        xext_ref[0:SUBLANES, :] = jnp.zeros((SUBLANES, D_RNN), F32)
        h_ref[...] = jnp.zeros_like(h_ref)

    xb = xy_ref[0, :, :D_RNN]
    xext_ref[SUBLANES:, :] = xb
    xc = cb_ref[...]
    for k in range(CONV_W):
        off = SUBLANES - (CONV_W - 1) + k
        xc = xc + xext_ref[off:off + tc, :] * cw_ref[k:k + 1, :]
    xext_ref[0:SUBLANES, :] = xb[tc - SUBLANES:, :]
    a, b = _lru_gates(xc, ga_w_ref, ga_b_ref[...], gx_w_ref, gx_b_ref[...], lam_ref[...], False)
    a_ref[...] = a
    b_ref[...] = b

    def step(t, h):
        h = a_ref[pl.ds(t, 1), :] * h + b_ref[pl.ds(t, 1), :]
        hs_ref[pl.ds(t, 1), :] = h
        return h

    h = lax.fori_loop(0, tc, step, h_ref[...], unroll=8)
    h_ref[...] = h
    hlast_ref[0] = h
    o_ref[0] = (hs_ref[...] * _gelu(xy_ref[0, :, D_RNN:])).astype(o_ref.dtype)


def lru_prompt(xy, cw, cb, ga_w, ga_b, gx_w, gx_b, lam):
    bsz, t, _ = xy.shape
    tc = LRU_CHUNK
    row = lambda v: v.reshape(1, D_RNN)
    full2 = lambda shp: pl.BlockSpec(shp, lambda b, c: (0, 0))
    full3 = lambda shp: pl.BlockSpec(shp, lambda b, c: (0, 0, 0))
    return pl.pallas_call(
        _lru_prompt_kernel,
        grid=(bsz, t // tc),
        in_specs=[pl.BlockSpec((1, tc, 2 * D_RNN), lambda b, c: (b, c, 0)),
                  full2((CONV_W, D_RNN)), full2((1, D_RNN)),
                  full3((LRU_BLOCKS, LRU_BLOCK_W, LRU_BLOCK_W)), full2((1, D_RNN)),
                  full3((LRU_BLOCKS, LRU_BLOCK_W, LRU_BLOCK_W)), full2((1, D_RNN)),
                  full2((1, D_RNN))],
        out_specs=[pl.BlockSpec((1, tc, D_RNN), lambda b, c: (b, c, 0)),
                   pl.BlockSpec((1, 1, D_RNN), lambda b, c: (b, 0, 0))],
        out_shape=[jax.ShapeDtypeStruct((bsz, t, D_RNN), BF16),
                   jax.ShapeDtypeStruct((bsz, 1, D_RNN), F32)],
        scratch_shapes=[pltpu.VMEM((tc + SUBLANES, D_RNN), F32),
                        pltpu.VMEM((tc, D_RNN), F32),
                        pltpu.VMEM((tc, D_RNN), F32),
                        pltpu.VMEM((tc, D_RNN), F32),
                        pltpu.VMEM((1, D_RNN), F32)],
        compiler_params=_cparams(("parallel", "arbitrary")),
        name="lru_prompt",
    )(xy, cw, row(cb), ga_w.astype(BF16), row(ga_b), gx_w.astype(BF16), row(gx_b), row(lam))


def _lru_step_kernel(xy_ref, cbuf_ref, h0_ref, cw_ref, cb_ref, ga_w_ref, ga_b_ref, gx_w_ref, gx_b_ref, lam_ref,
                     o_ref, h_ref):
    xb = xy_ref[:, :D_RNN]
    xc = cb_ref[...]
    for k in range(CONV_W - 1):
        xc = xc + cbuf_ref[k] * cw_ref[k:k + 1, :]
    xc = xc + xb * cw_ref[CONV_W - 1:CONV_W, :]
    a, b = _lru_gates(xc, ga_w_ref, ga_b_ref[...], gx_w_ref, gx_b_ref[...], lam_ref[...], True)
    h = a * h0_ref[...] + b
    h_ref[...] = h
    o_ref[...] = h * _gelu(xy_ref[:, D_RNN:])


def lru_step(xy, cbuf, h0, cw, cb, ga_w, ga_b, gx_w, gx_b, lam):
    bsz = xy.shape[0]
    row = lambda v: v.reshape(1, D_RNN)
    return pl.pallas_call(
        _lru_step_kernel,
        out_shape=[jax.ShapeDtypeStruct((bsz, D_RNN), F32),
                   jax.ShapeDtypeStruct((bsz, D_RNN), F32)],
        compiler_params=pltpu.CompilerParams(vmem_limit_bytes=VMEM_LIMIT),
        name="lru_step",
    )(xy, cbuf, h0, cw, row(cb), ga_w, row(ga_b), gx_w, row(gx_b), row(lam))


HALF = HEAD_DIM
ROWS8 = SUBLANES


def _lane_is_lo(shape):
    return lax.broadcasted_iota(jnp.int32, shape, len(shape) - 1) < HALF


def _pair_head_norm(y, gamma):
    sq = y * y
    lo = _lane_is_lo(y.shape)
    ss_lo = jnp.sum(jnp.where(lo, sq, 0.0), axis=-1, keepdims=True)
    ss_hi = jnp.sum(jnp.where(lo, 0.0, sq), axis=-1, keepdims=True)
    ms = jnp.where(lo, ss_lo, ss_hi) * (1.0 / HEAD_DIM)
    return y * lax.rsqrt(ms + EPS) * gamma


def _roll_half(x):
    return pltpu.roll(x, HALF, axis=x.ndim - 1)


def _online_update(carry, s, vblk):
    m_old, l_old, acc_old = carry
    m_new = jnp.maximum(m_old, jnp.max(s, axis=-1, keepdims=True))
    alpha = jnp.exp(m_old - m_new)
    e = jnp.exp(s - m_new)
    l_new = alpha * l_old + jnp.sum(e, axis=-1, keepdims=True)
    acc_new = alpha * acc_old + _dot_nn(e.astype(BF16), vblk)
    return m_new, l_new, acc_new


def _attend_extra(q8, keys, vals, kmask, k_new, v_new):
    q_parts = _split2(q8)
    s = _dot3_parts(q_parts, _split2(keys), _dot_nt)
    if kmask is not None:
        s = jnp.where(kmask, s, NEG)
    s_new = jnp.sum(q8 * k_new, axis=-1, keepdims=True)
    m = jnp.maximum(jnp.max(s, axis=-1, keepdims=True), s_new)
    e = jnp.exp(s - m)
    e_new = jnp.exp(s_new - m)
    den = jnp.sum(e, axis=-1, keepdims=True) + e_new
    num = _dot3_parts(_split2(e), _split2(vals), _dot_nn) + e_new * v_new
    return num, den, m


NSA_QD = NSA_HEADS * HEAD_DIM
NSA_KVD = NSA_KV_HEADS * HEAD_DIM
NSA_GATE_LANES = LANES
NSA_PROJ_COLS = NSA_QD + 6 * NSA_KVD + NSA_GATE_LANES


def _nsa_proj_kernel(x_ref, g_ref, w_ref, qn_ref, kn_ref, q_ref, gate_ref, rows_ref, win_ref, *, hp):
    y = _mm(_rms(x_ref[...], g_ref[...]), w_ref[...], hp)
    qn = qn_ref[...]
    for p in range(NSA_QD // LANES):
        blk = y[:, p * LANES:(p + 1) * LANES]
        q_ref[:, p * LANES:(p + 1) * LANES] = (_pair_head_norm(blk, qn) * (HEAD_DIM ** -0.5)).astype(q_ref.dtype)
    kv0 = NSA_QD
    for c in range(6):
        for p in range(NSA_KVD // LANES):
            lo = kv0 + c * NSA_KVD + p * LANES
            blk = y[:, lo:lo + LANES]
            if c == 2:
                blk = _pair_head_norm(blk, kn_ref[0:1, :])
            elif c == 4:
                blk = _pair_head_norm(blk, kn_ref[1:2, :])
            if c < 4:
                rows_ref[:, c * NSA_KVD + p * LANES:c * NSA_KVD + (p + 1) * LANES] = blk
            else:
                win_ref[:, (c - 4) * NSA_KVD + p * LANES:(c - 4) * NSA_KVD + (p + 1) * LANES] = blk
    gate_ref[...] = jax.nn.sigmoid(y[:, kv0 + 6 * NSA_KVD:])


def nsa_proj(x, g, w_in, q_norm, k_norm, hp=False):
    m, d = x.shape
    n_gate = 3 * NSA_HEADS
    w = jnp.concatenate([w_in[:, :NSA_QD], w_in[:, NSA_QD + n_gate:], w_in[:, NSA_QD:NSA_QD + n_gate],
                         jnp.zeros((d, NSA_GATE_LANES - n_gate), w_in.dtype)], axis=1).astype(_wdtype(hp))
    qn = jnp.tile(q_norm, 2).reshape(1, LANES)
    kn = jnp.stack([jnp.tile(k_norm[1], 2), jnp.tile(k_norm[2], 2)])
    tm = _tile(m)
    return pl.pallas_call(
        functools.partial(_nsa_proj_kernel, hp=hp),
        grid=(m // tm,),
        in_specs=[pl.BlockSpec((tm, d), lambda i: (i, 0)),
                  pl.BlockSpec((1, d), lambda i: (0, 0)),
                  pl.BlockSpec((d, NSA_PROJ_COLS), lambda i: (0, 0)),
                  pl.BlockSpec((1, LANES), lambda i: (0, 0)),
                  pl.BlockSpec((2, LANES), lambda i: (0, 0))],
        out_specs=[pl.BlockSpec((tm, NSA_QD), lambda i: (i, 0)),
                   pl.BlockSpec((tm, NSA_GATE_LANES), lambda i: (i, 0)),
                   pl.BlockSpec((tm, 4 * NSA_KVD), lambda i: (i, 0)),
                   pl.BlockSpec((tm, 2 * NSA_KVD), lambda i: (i, 0))],
        out_shape=[jax.ShapeDtypeStruct((m, NSA_QD), _wdtype(hp)),
                   jax.ShapeDtypeStruct((m, NSA_GATE_LANES), F32),
                   jax.ShapeDtypeStruct((m, 4 * NSA_KVD), F32),
                   jax.ShapeDtypeStruct((m, 2 * NSA_KVD), F32)],
        compiler_params=_cparams(("parallel",)),
        name="nsa_proj",
    )(x, g.reshape(1, d), w, qn, kn)


def _cmp_weights(w1, b1, w2, b2):
    eye = jnp.eye(NSA_KV_HEADS, dtype=w1.dtype)
    lead_tail = jnp.concatenate([w1[:, :CMP_STRIDE], w1[:, CMP_STRIDE:]], axis=-1)
    w1bd = jnp.einsum('csdh,kj->cskdjh', lead_tail, eye).reshape(2, CMP_STRIDE, NSA_KVD, NSA_KV_HEADS * 2 * PHI_HIDDEN)
    w2bd = jnp.einsum('chd,kj->ckhjd', w2, eye).reshape(2, NSA_KV_HEADS * PHI_HIDDEN, NSA_KVD)
    b1t = jnp.tile(b1, (1, NSA_KV_HEADS)).reshape(2, 1, NSA_KV_HEADS * PHI_HIDDEN)
    b2t = jnp.tile(b2, (1, NSA_KV_HEADS)).reshape(2, 1, NSA_KVD)
    return w1bd, b1t, w2bd, b2t


def _compress_halves(first_layer, b1, second_layer, b2, knorm, is_key, extra_tail=None):
    acc = first_layer()
    n_half = acc.shape[0]
    hids = []
    for k in range(NSA_KV_HEADS):
        lead = acc[:, k * 2 * PHI_HIDDEN:k * 2 * PHI_HIDDEN + PHI_HIDDEN]
        tail = acc[:, k * 2 * PHI_HIDDEN + PHI_HIDDEN:(k + 1) * 2 * PHI_HIDDEN]
        nxt = pltpu.roll(tail, n_half - 1, axis=0)
        if extra_tail is not None:
            row = lax.broadcasted_iota(jnp.int32, nxt.shape, 0)
            nxt = jnp.where(row == n_half - 1, extra_tail[:, k * PHI_HIDDEN:(k + 1) * PHI_HIDDEN], nxt)
        hids.append(lead + nxt)
    hid = _gelu(jnp.concatenate(hids, axis=-1) + b1)
    out = second_layer(hid) + b2
    normed = jnp.concatenate([_pair_head_norm(out[:, p * LANES:(p + 1) * LANES], knorm)
                              for p in range(NSA_KVD // LANES)], axis=-1)
    return jnp.where(is_key, normed, out)


def _cmp_prompt_kernel(x_ref, w1_ref, b1_ref, w2_ref, b2_ref, kn_ref, o_ref):
    c = pl.program_id(1)

    def first_layer():
        acc = None
        for s in range(CMP_STRIDE):
            part = _dot_nn(x_ref[:, s, :].astype(BF16), w1_ref[0, s])
            acc = part if acc is None else acc + part
        return acc

    out = _compress_halves(first_layer, b1_ref[0], lambda hid: _dot_nn(hid.astype(BF16), w2_ref[0]), b2_ref[0],
                           kn_ref[...], c == 0)
    o_ref[0, 0] = out.astype(o_ref.dtype)


def cmp_prompt(rows, n_b, seq, k_cmp_norm, w1, b1, w2, b2):
    n_half = seq // CMP_STRIDE
    x3 = rows.reshape(rows.shape[0] // CMP_STRIDE, CMP_STRIDE, 4 * NSA_KVD)
    w1bd, b1t, w2bd, b2t = _cmp_weights(w1, b1, w2, b2)
    kn = jnp.tile(k_cmp_norm, 2).reshape(1, LANES)
    return pl.pallas_call(
        _cmp_prompt_kernel,
        grid=(n_b, 2),
        in_specs=[pl.BlockSpec((n_half, CMP_STRIDE, NSA_KVD), lambda b, c: (b, 0, c)),
                  pl.BlockSpec((1, CMP_STRIDE, NSA_KVD, w1bd.shape[-1]), lambda b, c: (c, 0, 0, 0)),
                  pl.BlockSpec((1, 1, b1t.shape[-1]), lambda b, c: (c, 0, 0)),
                  pl.BlockSpec((1, w2bd.shape[1], NSA_KVD), lambda b, c: (c, 0, 0)),
                  pl.BlockSpec((1, 1, NSA_KVD), lambda b, c: (c, 0, 0)),
                  pl.BlockSpec((1, LANES), lambda b, c: (0, 0))],
        out_specs=pl.BlockSpec((1, 1, n_half, NSA_KVD), lambda b, c: (b, c, 0, 0)),
        out_shape=jax.ShapeDtypeStruct((n_b, 2, n_half, NSA_KVD), BF16),
        compiler_params=_cparams(("parallel", "arbitrary")),
        name="nsa_cmp_prompt",
    )(x3, w1bd.astype(BF16), b1t, w2bd.astype(BF16), b2t, kn)


NSA_TQ = 128
NSA_KC = 512
NSA_WCHUNK = NSA_WINDOW + NSA_TQ


def _sel_map(n_sel, n_cmp, rows, cols):
    ratio = SEL_BLOCK // CMP_STRIDE
    span = CMP_BLOCK // CMP_STRIDE
    idx = np.arange(n_sel)[:, None] * ratio - (span - 1) + np.arange(ratio + span - 1)[None, :]
    wgt = np.convolve(np.ones(ratio), np.ones(span))[None, :] * ((idx >= 0) & (idx < n_cmp))
    m = np.zeros((rows, cols), np.float32)
    for j in range(n_sel):
        for a in range(idx.shape[1]):
            if wgt[j, a] != 0:
                m[idx[j, a], j] += wgt[j, a]
    return m


def _topk_mask_cols(score_t, n_top):
    n = score_t.shape[0]
    row = lax.broadcasted_iota(jnp.int32, score_t.shape, 0)
    rank = jnp.zeros(score_t.shape, F32)
    for i in range(n):
        ri = score_t[i:i + 1, :]
        beats = jnp.where(ri > score_t, 1.0, jnp.where((ri == score_t) & (row > i), 1.0, 0.0))
        rank = rank + beats
    return jnp.where(rank < n_top, 1.0, 0.0)


def _nsa_prompt_kernel(q_ref, gate_ref, kc_ref, vc_ref, ks_ref, vs_ref, kw_ref, vw_ref, map_ref, exp_ref,
                       o_ref, ks_bf, vs_bf, kw_bf, vw_bf):
    kv = pl.program_id(1)
    i = pl.program_id(2)
    tq = NSA_TQ
    seq = ks_ref.shape[0]
    n_sel = seq // SEL_BLOCK

    @pl.when(i == 0)
    def _():
        ks_bf[...] = ks_ref[...].astype(BF16)
        vs_bf[...] = vs_ref[...].astype(BF16)
        kw_bf[...] = kw_ref[...].astype(BF16)
        vw_bf[...] = vw_ref[...].astype(BF16)

    hv = kv % 2
    t0 = i * tq
    t = t0 + lax.broadcasted_iota(jnp.int32, (tq, 1), 0)
    lo128 = _lane_is_lo((tq, LANES))

    qs = []
    for g in range(NSA_GROUP):
        pair = q_ref[:, (g // 2) * LANES:(g // 2 + 1) * LANES].astype(F32)
        own = jnp.where(lo128 if g % 2 == 0 else ~lo128, pair, 0.0)
        qs.append(jnp.where(hv == g % 2, own, _roll_half(own)).astype(BF16))

    gates = gate_ref[...]
    glane = lax.broadcasted_iota(jnp.int32, gates.shape, 1)

    def gate_col(g, c):
        idx = (kv * NSA_GROUP + g) * 3 + c
        return jnp.sum(jnp.where(glane == idx, gates, 0.0), axis=-1, keepdims=True)

    kc = kc_ref[0, 0]
    vc = vc_ref[0, 0]
    n_cmp_pad = kc.shape[0]
    cmp_end = lax.broadcasted_iota(jnp.int32, (tq, n_cmp_pad), 1) * CMP_STRIDE + (CMP_BLOCK - 1)
    cmask = cmp_end <= t
    o_c = []
    imp = None
    for g in range(NSA_GROUP):
        s = jnp.where(cmask, _dot_nt(qs[g], kc), NEG)
        e = jnp.where(cmask, jnp.exp(s - jnp.max(s, axis=-1, keepdims=True)), 0.0)
        p = e / jnp.maximum(jnp.sum(e, axis=-1, keepdims=True), TINY)
        o_c.append(_dot_nn(p.astype(BF16), vc))
        imp = p if imp is None else imp + p

    p_blk = None
    for term in _split3(imp):
        part = _dot_nn(term, map_ref[...])
        p_blk = part if p_blk is None else p_blk + part
    blk = lax.broadcasted_iota(jnp.int32, (tq, n_cmp_pad), 1)
    cur = t // SEL_BLOCK
    forced = (blk == 0) | (blk == cur) | (blk == cur - 1)
    score = jnp.where(blk * SEL_BLOCK > t, NEG, jnp.where(forced, BIG, p_blk))
    score_t = score.T[0:n_sel, :]
    sel_t = _topk_mask_cols(score_t, min(SEL_TOP, n_sel))
    sel_t = jnp.concatenate([sel_t, jnp.zeros((n_cmp_pad - n_sel, tq), F32)], axis=0)
    sel = sel_t.T.astype(BF16)

    kcs = NSA_KC
    n_chunks = (t0 + tq + kcs - 1) // kcs

    def chunk(c, carry):
        c0 = pl.multiple_of(c * kcs, kcs)
        kblk = ks_bf[pl.ds(c0, kcs), :]
        vblk = vs_bf[pl.ds(c0, kcs), :]
        maskf = _dot_nn(sel, exp_ref[:, pl.ds(c0, kcs)])
        kpos = c0 + lax.broadcasted_iota(jnp.int32, (tq, kcs), 1)
        valid = jnp.where(kpos <= t, maskf, 0.0) > 0.5
        return tuple(_online_update(carry[g], jnp.where(valid, _dot_nt(qs[g], kblk), NEG), vblk)
                     for g in range(NSA_GROUP))

    init = tuple((jnp.full((tq, 1), NEG, F32), jnp.zeros((tq, 1), F32), jnp.zeros((tq, LANES), F32))
                 for _ in range(NSA_GROUP))
    fin = lax.fori_loop(0, n_chunks, chunk, init)

    ws = pl.multiple_of(jnp.maximum(t0 - NSA_WINDOW, 0), tq)
    kwin = kw_bf[pl.ds(ws, NSA_WCHUNK), :]
    vwin = vw_bf[pl.ds(ws, NSA_WCHUNK), :]
    dist = t - (ws + lax.broadcasted_iota(jnp.int32, (tq, NSA_WCHUNK), 1))
    wmask = (dist >= 0) & (dist < NSA_WINDOW)

    outs = []
    for g in range(NSA_GROUP):
        _, l_s, acc_s = fin[g]
        o_s = acc_s / l_s
        s = jnp.where(wmask, _dot_nt(qs[g], kwin), NEG)
        e = jnp.where(wmask, jnp.exp(s - jnp.max(s, axis=-1, keepdims=True)), 0.0)
        p = e / jnp.maximum(jnp.sum(e, axis=-1, keepdims=True), TINY)
        o_w = _dot_nn(p.astype(BF16), vwin)
        outs.append(gate_col(g, 0) * o_c[g] + gate_col(g, 1) * o_s + gate_col(g, 2) * o_w)

    for j in range(NSA_GROUP // 2):
        a = jnp.where(hv == 0, outs[2 * j], _roll_half(outs[2 * j]))
        b = jnp.where(hv == 1, outs[2 * j + 1], _roll_half(outs[2 * j + 1]))
        o_ref[:, j * LANES:(j + 1) * LANES] = jnp.where(lo128, a, b).astype(o_ref.dtype)


def nsa_prompt_attention(q, gates, cmp, rows, win, n_b, seq):
    tq = NSA_TQ
    nq = seq // tq
    n_cmp_pad = seq // CMP_STRIDE
    assert n_cmp_pad == LANES and seq % NSA_KC == 0
    n_sel = seq // SEL_BLOCK
    smap = jnp.asarray(_sel_map(n_sel, n_cmp_pad - 1, n_cmp_pad, n_cmp_pad), BF16)
    expand = jnp.asarray((np.arange(n_cmp_pad)[:, None] == (np.arange(seq)[None, :] // SEL_BLOCK)), BF16)
    kvspec = lambda col0: pl.BlockSpec((seq, LANES), lambda b, k, i: (b, col0 + k // 2))
    return pl.pallas_call(
        _nsa_prompt_kernel,
        grid=(n_b, NSA_KV_HEADS, nq),
        in_specs=[pl.BlockSpec((tq, NSA_GROUP * HEAD_DIM), lambda b, k, i: (b * nq + i, k)),
                  pl.BlockSpec((tq, NSA_GATE_LANES), lambda b, k, i: (b * nq + i, 0)),
                  pl.BlockSpec((1, 1, n_cmp_pad, LANES), lambda b, k, i: (b, 0, 0, k // 2)),
                  pl.BlockSpec((1, 1, n_cmp_pad, LANES), lambda b, k, i: (b, 1, 0, k // 2)),
                  kvspec(4), kvspec(6), kvspec(0), kvspec(2),
                  pl.BlockSpec((n_cmp_pad, n_cmp_pad), lambda b, k, i: (0, 0)),
                  pl.BlockSpec((n_cmp_pad, seq), lambda b, k, i: (0, 0))],
        out_specs=pl.BlockSpec((tq, NSA_GROUP * HEAD_DIM), lambda b, k, i: (b * nq + i, k)),
        out_shape=jax.ShapeDtypeStruct((n_b * seq, NSA_QD), BF16),
        scratch_shapes=[pltpu.VMEM((seq, LANES), BF16)] * 4,
        compiler_params=_cparams(("arbitrary", "arbitrary", "arbitrary")),
        name="nsa_prompt_attention",
    )(q, gates, cmp, cmp, rows, rows, win, win, smap, expand)


HALVES_PER_PAGE = PAGE_SIZE // CMP_STRIDE
SEL_PER_PAGE = PAGE_SIZE // SEL_BLOCK
N_SEL_PAST = SEL_TOP - 1


def _cmp_sample_kernel(pt_ref, cache_ref, new_ref, w1h_ref, w1l_ref, b1_ref, w2h_ref, w2l_ref, b2_ref, kn_ref,
                       o_ref, buf, sem):
    b = pl.program_id(0)
    c = pl.program_id(1)
    n_pages = pt_ref.shape[1]
    col0 = pl.multiple_of(c * NSA_KVD, NSA_KVD)

    def page_copy(p):
        return pltpu.make_async_copy(
            cache_ref.at[pt_ref[b, p], :, :, pl.ds(col0, NSA_KVD)],
            buf.at[pl.ds(p * HALVES_PER_PAGE, HALVES_PER_PAGE)], sem)

    for p in range(n_pages):
        page_copy(p).start()
    new_row = jnp.broadcast_to(new_ref[0], (ROWS8, NSA_KVD))
    t_all = _dot3_parts(_split2(new_row), (w1h_ref[0, 0], w1l_ref[0, 0]), _dot_nn)[0:1]
    extra = jnp.concatenate([t_all[:, k * 2 * PHI_HIDDEN + PHI_HIDDEN:(k + 1) * 2 * PHI_HIDDEN]
                             for k in range(NSA_KV_HEADS)], axis=-1)
    for p in range(n_pages):
        page_copy(p).wait()

    def first_layer():
        acc = None
        for s in range(CMP_STRIDE):
            part = _dot3_parts(_split2(buf[:, s, :]), (w1h_ref[0, s], w1l_ref[0, s]), _dot_nn)
            acc = part if acc is None else acc + part
        return acc

    out = _compress_halves(first_layer, b1_ref[0],
                           lambda hid: _dot3_parts(_split2(hid), (w2h_ref[0], w2l_ref[0]), _dot_nn), b2_ref[0],
                           kn_ref[...], c == 0, extra_tail=extra)
    o_ref[0, 0] = out


def cmp_sample(cache, page_table, rows_new, k_cmp_norm, w1, b1, w2, b2):
    n_b, n_pages = page_table.shape
    n_half = n_pages * HALVES_PER_PAGE
    cache4 = cache.reshape(cache.shape[0], HALVES_PER_PAGE, CMP_STRIDE, 4 * NSA_KVD)
    new3 = rows_new.reshape(n_b, 1, 4 * NSA_KVD)
    w1bd, b1t, w2bd, b2t = _cmp_weights(w1, b1, w2, b2)
    w1h, w1l = _split2_masked(w1bd)
    w2h, w2l = _split2_masked(w2bd)
    kn = jnp.tile(k_cmp_norm, 2).reshape(1, LANES)
    w1spec = pl.BlockSpec((1, CMP_STRIDE, NSA_KVD, w1bd.shape[-1]), lambda b, c, pt: (c, 0, 0, 0))
    w2spec = pl.BlockSpec((1, w2bd.shape[1], NSA_KVD), lambda b, c, pt: (c, 0, 0))
    grid_spec = pltpu.PrefetchScalarGridSpec(
        num_scalar_prefetch=1,
        grid=(n_b, 2),
        in_specs=[pl.BlockSpec(memory_space=pl.ANY),
                  pl.BlockSpec((1, 1, NSA_KVD), lambda b, c, pt: (b, 0, c)),
                  w1spec, w1spec,
                  pl.BlockSpec((1, 1, b1t.shape[-1]), lambda b, c, pt: (c, 0, 0)),
                  w2spec, w2spec,
                  pl.BlockSpec((1, 1, NSA_KVD), lambda b, c, pt: (c, 0, 0)),
                  pl.BlockSpec((1, LANES), lambda b, c, pt: (0, 0))],
        out_specs=pl.BlockSpec((1, 1, n_half, NSA_KVD), lambda b, c, pt: (b, c, 0, 0)),
        scratch_shapes=[pltpu.VMEM((n_half, CMP_STRIDE, NSA_KVD), F32),
                        pltpu.SemaphoreType.DMA(())],
    )
    return pl.pallas_call(
        _cmp_sample_kernel,
        grid_spec=grid_spec,
        out_shape=jax.ShapeDtypeStruct((n_b, 2, n_half, NSA_KVD), F32),
        compiler_params=_cparams(("arbitrary", "arbitrary")),
        name="nsa_cmp_sample",
    )(page_table, cache4, new3, w1h, w1l, b1t, w2h, w2l, b2t, kn)


def _group_queries(q_row, kv):
    lo = _lane_is_lo((ROWS8, LANES))
    row = lax.broadcasted_iota(jnp.int32, (ROWS8, LANES), 0)
    out = jnp.zeros((ROWS8, LANES), F32)
    for g in range(NSA_GROUP):
        h = kv * NSA_GROUP + g
        pair = jnp.broadcast_to(q_row[:, (h // 2) * LANES:(h // 2 + 1) * LANES], (ROWS8, LANES))
        own = jnp.where(lo if h % 2 == 0 else ~lo, pair, 0.0)
        own = own if h % 2 == kv % 2 else _roll_half(own)
        out = jnp.where(row == g, own, out)
    return out


def _sel_sample_kernel(q_ref, cmp_ref, map_ref, oc_ref, flag_ref, *, t_pos, n_sel):
    q_row = q_ref[0]
    n_cmp = cmp_ref.shape[2]
    pad = map_ref.shape[1]
    cmask = (lax.broadcasted_iota(jnp.int32, (ROWS8, n_cmp), 1) * CMP_STRIDE + (CMP_BLOCK - 1)) <= t_pos
    lane = lax.broadcasted_iota(jnp.int32, (1, pad), 1)
    cur = t_pos // SEL_BLOCK
    forced = (lane == 0) | (lane == cur) | (lane == cur - 1)
    eye = lax.broadcasted_iota(jnp.int32, (pad, pad), 0) == lax.broadcasted_iota(jnp.int32, (pad, pad), 1)
    before = lax.broadcasted_iota(jnp.int32, (pad, pad), 0) < lax.broadcasted_iota(jnp.int32, (pad, pad), 1)
    for kv in range(NSA_KV_HEADS):
        q8 = _group_queries(q_row, kv)
        kc = cmp_ref[0, 0, :, (kv // 2) * LANES:(kv // 2 + 1) * LANES]
        vc = cmp_ref[0, 1, :, (kv // 2) * LANES:(kv // 2 + 1) * LANES]
        s = jnp.where(cmask, _dot3_parts(_split2(q8), _split2(kc), _dot_nt), NEG)
        e = jnp.where(cmask, jnp.exp(s - jnp.max(s, axis=-1, keepdims=True)), 0.0)
        p = e / jnp.maximum(jnp.sum(e, axis=-1, keepdims=True), TINY)
        oc_ref[0, kv] = _dot3_parts(_split2(p), _split2(vc), _dot_nn)
        imp = p[0:1]
        for g in range(1, NSA_GROUP):
            imp = imp + p[g:g + 1]
        imp8 = jnp.broadcast_to(imp, (ROWS8, n_cmp))
        p_blk = None
        for term in _split3(imp8):
            part = _dot_nn(term, map_ref[...])
            p_blk = part if p_blk is None else p_blk + part
        score = jnp.where(lane * SEL_BLOCK > t_pos, NEG, jnp.where(forced, BIG, p_blk[0:1]))
        s_row = jnp.broadcast_to(score, (pad, pad))
        s_col = jnp.sum(jnp.where(eye, s_row, 0.0), axis=1, keepdims=True)
        beats = jnp.where(s_col > s_row, 1.0, jnp.where((s_col == s_row) & before, 1.0, 0.0))
        rank = jnp.sum(beats, axis=0, keepdims=True)
        chosen = (rank < min(SEL_TOP, n_sel)) & (lane < n_sel - 1)
        flag_ref[0, kv] = jnp.broadcast_to(jnp.where(chosen, 1, 0).astype(jnp.int32), (ROWS8, pad))


def sel_sample(q_s, cmp_s, t_pos):
    n_b = q_s.shape[0]
    n_cmp = cmp_s.shape[2]
    n_sel = -(-(t_pos + 1) // SEL_BLOCK)
    assert n_sel - 1 >= SEL_TOP
    pad = -(-n_sel // LANES) * LANES
    smap = jnp.asarray(_sel_map(n_sel, n_cmp, n_cmp, pad), BF16)
    kern = functools.partial(_sel_sample_kernel, t_pos=t_pos, n_sel=n_sel)
    return pl.pallas_call(
        kern,
        grid=(n_b,),
        in_specs=[pl.BlockSpec((1, 1, NSA_QD), lambda b: (b, 0, 0)),
                  pl.BlockSpec((1, 2, n_cmp, NSA_KVD), lambda b: (b, 0, 0, 0)),
                  pl.BlockSpec((n_cmp, pad), lambda b: (0, 0))],
        out_specs=[pl.BlockSpec((1, NSA_KV_HEADS, ROWS8, LANES), lambda b: (b, 0, 0, 0)),
                   pl.BlockSpec((1, NSA_KV_HEADS, ROWS8, pad), lambda b: (b, 0, 0, 0))],
        out_shape=[jax.ShapeDtypeStruct((n_b, NSA_KV_HEADS, ROWS8, LANES), F32),
                   jax.ShapeDtypeStruct((n_b, NSA_KV_HEADS, ROWS8, pad), jnp.int32)],
        compiler_params=_cparams(("parallel",)),
        name="nsa_sel_sample",
    )(q_s.reshape(n_b, 1, NSA_QD), cmp_s, smap)


def _attn_sample_kernel(pt_ref, flag_ref, q_ref, gate_ref, oc_ref, new_rows_ref, new_win_ref, wcache_ref, cache_ref,
                        o_ref, kbuf, vbuf, sem):
    b = pl.program_id(0)
    n_past_blocks = pt_ref.shape[1] * SEL_PER_PAGE

    for kv in range(NSA_KV_HEADS):
        def issue(j, slot):
            @pl.when(flag_ref[0, kv, j] != 0)
            def _():
                page = pt_ref[b, j // SEL_PER_PAGE]
                r0 = pl.multiple_of((j % SEL_PER_PAGE) * SEL_BLOCK, SEL_BLOCK)
                dst = jnp.minimum(slot, N_SEL_PAST - 1)
                pltpu.make_async_copy(cache_ref.at[page, pl.ds(r0, SEL_BLOCK), 2 * 2 + kv // 2, :],
                                      kbuf.at[kv, dst], sem.at[kv]).start()
                pltpu.make_async_copy(cache_ref.at[page, pl.ds(r0, SEL_BLOCK), 3 * 2 + kv // 2, :],
                                      vbuf.at[kv, dst], sem.at[kv]).start()
            return slot + jnp.where(flag_ref[0, kv, j] != 0, 1, 0)

        lax.fori_loop(0, n_past_blocks, issue, jnp.int32(0))

    q_row = q_ref[0]
    gates = jnp.broadcast_to(gate_ref[0], (ROWS8, NSA_GATE_LANES))
    grow = lax.broadcasted_iota(jnp.int32, gates.shape, 0)
    glane = lax.broadcasted_iota(jnp.int32, gates.shape, 1)
    lo = _lane_is_lo((1, LANES))
    new_rows = new_rows_ref[0]
    new_win = new_win_ref[0]
    wrow = lax.broadcasted_iota(jnp.int32, (ROWS8, wcache_ref.shape[1]), 1)

    outs = []
    for kv in range(NSA_KV_HEADS):
        pr = kv // 2
        q8 = _group_queries(q_row, kv)
        for _ in range(2 * N_SEL_PAST):
            pltpu.make_async_copy(cache_ref.at[0, pl.ds(0, SEL_BLOCK), 0, :], kbuf.at[kv, 0], sem.at[kv]).wait()
        keys = kbuf[kv].reshape(N_SEL_PAST * SEL_BLOCK, LANES)
        vals = vbuf[kv].reshape(N_SEL_PAST * SEL_BLOCK, LANES)
        num, den, _ = _attend_extra(q8, keys, vals, None,
                                    new_rows[:, (2 * 2 + pr) * LANES:(2 * 2 + pr + 1) * LANES],
                                    new_rows[:, (3 * 2 + pr) * LANES:(3 * 2 + pr + 1) * LANES])
        o_s = num / den
        num, den, _ = _attend_extra(q8, wcache_ref[0, :, pr * LANES:(pr + 1) * LANES],
                                    wcache_ref[0, :, (2 + pr) * LANES:(2 + pr + 1) * LANES], wrow >= 1,
                                    new_win[:, pr * LANES:(pr + 1) * LANES],
                                    new_win[:, (2 + pr) * LANES:(2 + pr + 1) * LANES])
        o_w = num / den

        def gate_col(c):
            idx = (kv * NSA_GROUP + grow) * 3 + c
            return jnp.sum(jnp.where((glane == idx) & (grow < NSA_GROUP), gates, 0.0), axis=-1, keepdims=True)

        outs.append(gate_col(0) * oc_ref[0, kv] + gate_col(1) * o_s + gate_col(2) * o_w)

    for kv in range(NSA_KV_HEADS):
        for jj in range(NSA_GROUP // 2):
            a = outs[kv][2 * jj:2 * jj + 1]
            bb = outs[kv][2 * jj + 1:2 * jj + 2]
            a = a if kv % 2 == 0 else _roll_half(a)
            bb = bb if kv % 2 == 1 else _roll_half(bb)
            col = (kv * NSA_GROUP + 2 * jj) // 2
            o_ref[0, :, col * LANES:(col + 1) * LANES] = jnp.where(lo, a, bb)


def attn_sample(q_s, gates_s, o_cmp, flags, rows_new, win_new, win_cache, cache, page_table):
    n_b = q_s.shape[0]
    n_win = win_cache.shape[1]
    assert n_win == NSA_WINDOW
    cache4 = cache.reshape(cache.shape[0], PAGE_SIZE, 4 * NSA_KVD // LANES, LANES)
    wc = win_cache.reshape(n_b, n_win, 2 * NSA_KVD)
    pad = flags.shape[-1]
    grid_spec = pltpu.PrefetchScalarGridSpec(
        num_scalar_prefetch=1,
        grid=(n_b,),
        in_specs=[pl.BlockSpec((1, NSA_KV_HEADS, pad), lambda b, pt: (b, 0, 0), memory_space=pltpu.SMEM),
                  pl.BlockSpec((1, 1, NSA_QD), lambda b, pt: (b, 0, 0)),
                  pl.BlockSpec((1, 1, NSA_GATE_LANES), lambda b, pt: (b, 0, 0)),
                  pl.BlockSpec((1, NSA_KV_HEADS, ROWS8, LANES), lambda b, pt: (b, 0, 0, 0)),
                  pl.BlockSpec((1, 1, 4 * NSA_KVD), lambda b, pt: (b, 0, 0)),
                  pl.BlockSpec((1, 1, 2 * NSA_KVD), lambda b, pt: (b, 0, 0)),
                  pl.BlockSpec((1, n_win, 2 * NSA_KVD), lambda b, pt: (b, 0, 0)),
                  pl.BlockSpec(memory_space=pl.ANY)],
        out_specs=pl.BlockSpec((1, 1, NSA_QD), lambda b, pt: (b, 0, 0)),
        scratch_shapes=[pltpu.VMEM((NSA_KV_HEADS, N_SEL_PAST, SEL_BLOCK, LANES), F32),
                        pltpu.VMEM((NSA_KV_HEADS, N_SEL_PAST, SEL_BLOCK, LANES), F32),
                        pltpu.SemaphoreType.DMA((NSA_KV_HEADS,))],
    )
    out = pl.pallas_call(
        _attn_sample_kernel,
        grid_spec=grid_spec,
        out_shape=jax.ShapeDtypeStruct((n_b, 1, NSA_QD), F32),
        compiler_params=_cparams(("arbitrary",)),
        name="nsa_attn_sample",
    )(page_table, flags[:, :, 0, :], q_s.reshape(n_b, 1, NSA_QD), gates_s.reshape(n_b, 1, NSA_GATE_LANES), o_cmp,
      rows_new.reshape(n_b, 1, 4 * NSA_KVD), win_new.reshape(n_b, 1, 2 * NSA_KVD), wc, cache4)
    return out.reshape(n_b, NSA_QD)


DIL_QD = DIL_GROUPS * DIL_HEADS * HEAD_DIM
DIL_GD = DIL_HEADS * HEAD_DIM
DIL_TQ = 128
DIL_KC = 512


def _dil_proj_kernel(x_ref, g_ref, w_ref, qn_ref, kn_ref, q_ref, kv_ref, *, hp):
    y = _mm(_rms(x_ref[...], g_ref[...]), w_ref[...], hp)
    for p in range(DIL_QD // LANES):
        blk = y[:, p * LANES:(p + 1) * LANES]
        q_ref[:, p * LANES:(p + 1) * LANES] = (_pair_head_norm(blk, qn_ref[...]) * (HEAD_DIM ** -0.5)).astype(q_ref.dtype)
    for g in range(DIL_GROUPS):
        for p in range(DIL_GD // LANES):
            kblk = y[:, DIL_QD + g * DIL_GD + p * LANES:DIL_QD + g * DIL_GD + (p + 1) * LANES]
            vblk = y[:, 2 * DIL_QD + g * DIL_GD + p * LANES:2 * DIL_QD + g * DIL_GD + (p + 1) * LANES]
            kv_ref[:, g * 2 * DIL_GD + p * LANES:g * 2 * DIL_GD + (p + 1) * LANES] = _pair_head_norm(kblk, kn_ref[...])
            kv_ref[:, (g * 2 + 1) * DIL_GD + p * LANES:(g * 2 + 1) * DIL_GD + (p + 1) * LANES] = vblk


def dil_proj(x, g, w_in, q_norm, k_norm, hp=False):
    m, d = x.shape
    tm = _tile(m)
    qn = jnp.tile(q_norm, 2).reshape(1, LANES)
    kn = jnp.tile(k_norm, 2).reshape(1, LANES)
    return pl.pallas_call(
        functools.partial(_dil_proj_kernel, hp=hp),
        grid=(m // tm,),
        in_specs=[pl.BlockSpec((tm, d), lambda i: (i, 0)),
                  pl.BlockSpec((1, d), lambda i: (0, 0)),
                  pl.BlockSpec((d, 3 * DIL_QD), lambda i: (0, 0)),
                  pl.BlockSpec((1, LANES), lambda i: (0, 0)),
                  pl.BlockSpec((1, LANES), lambda i: (0, 0))],
        out_specs=[pl.BlockSpec((tm, DIL_QD), lambda i: (i, 0)),
                   pl.BlockSpec((tm, 2 * DIL_QD), lambda i: (i, 0))],
        out_shape=[jax.ShapeDtypeStruct((m, DIL_QD), _wdtype(hp)),
                   jax.ShapeDtypeStruct((m, 2 * DIL_QD), F32)],
        compiler_params=_cparams(("parallel",)),
        name="dil_proj",
    )(x, g.reshape(1, d), w_in.astype(_wdtype(hp)), qn, kn)


def _dil_prompt_kernel(q0_ref, q1_ref, q2_ref, k0_ref, v0_ref, k1_ref, v1_ref, k2_ref, v2_ref, o_ref,
                       k0_bf, v0_bf, k1_bf, v1_bf, k2_bf, v2_bf):
    i = pl.program_id(2)
    tq = DIL_TQ

    @pl.when(i == 0)
    def _():
        for src, dst in ((k0_ref, k0_bf), (v0_ref, v0_bf), (k1_ref, k1_bf), (v1_ref, v1_bf), (k2_ref, k2_bf), (v2_ref, v2_bf)):
            dst[...] = src[...].astype(BF16)

    t0 = i * tq
    t = t0 + lax.broadcasted_iota(jnp.int32, (tq, 1), 0)
    lo128 = _lane_is_lo((tq, LANES))
    q_refs = (q0_ref, q1_ref, q2_ref)
    k_bfs = (k0_bf, k1_bf, k2_bf)
    v_bfs = (v0_bf, v1_bf, v2_bf)

    def dil_mask(start, n, w, r):
        d = t - (start + lax.broadcasted_iota(jnp.int32, (tq, n), 1))
        return (d >= 0) & (d <= w) & ((d & (r - 1)) == 0)

    halves = []
    for hh in range(2):
        sel = lo128 if hh == 0 else ~lo128
        qh = [jnp.where(sel, q_refs[g][...].astype(F32), 0.0).astype(BF16) for g in range(DIL_GROUPS)]
        carry = (jnp.full((tq, 1), NEG, F32), jnp.zeros((tq, 1), F32), jnp.zeros((tq, LANES), F32))
        for g in range(DIL_GROUPS - 1):
            w, r = DIL_PAIRS[g]
            n = w + tq
            start = pl.multiple_of(jnp.maximum(t0 - w, 0), tq)
            s = jnp.where(dil_mask(start, n, w, r), _dot_nt(qh[g], k_bfs[g][pl.ds(start, n), :]), NEG)
            carry = _online_update(carry, s, v_bfs[g][pl.ds(start, n), :])
        w, r = DIL_PAIRS[DIL_GROUPS - 1]
        g = DIL_GROUPS - 1

        def chunk(c, carry):
            c0 = pl.multiple_of(c * DIL_KC, DIL_KC)
            s = jnp.where(dil_mask(c0, DIL_KC, w, r), _dot_nt(qh[g], k_bfs[g][pl.ds(c0, DIL_KC), :]), NEG)
            return _online_update(carry, s, v_bfs[g][pl.ds(c0, DIL_KC), :])

        _, l_f, acc_f = lax.fori_loop(0, (t0 + tq + DIL_KC - 1) // DIL_KC, chunk, carry)
        halves.append(acc_f / l_f)
    o_ref[...] = jnp.where(lo128, halves[0], halves[1]).astype(o_ref.dtype)


def dil_prompt_attention(q, kv, n_b, seq):
    tq = DIL_TQ
    nq = seq // tq
    n_pairs = DIL_GD // LANES
    assert DIL_PAIRS[-1][0] >= seq and seq % DIL_KC == 0
    qspec = lambda g: pl.BlockSpec((tq, LANES), lambda b, p, i: (b * nq + i, g * n_pairs + p))
    kvspec = lambda j: pl.BlockSpec((seq, LANES), lambda b, p, i: (b, j * n_pairs + p))
    return pl.pallas_call(
        _dil_prompt_kernel,
        grid=(n_b, n_pairs, nq),
        in_specs=[qspec(0), qspec(1), qspec(2)] + [kvspec(j) for j in range(2 * DIL_GROUPS)],
        out_specs=pl.BlockSpec((tq, LANES), lambda b, p, i: (b * nq + i, p)),
        out_shape=jax.ShapeDtypeStruct((n_b * seq, DIL_GD), BF16),
        scratch_shapes=[pltpu.VMEM((seq, LANES), BF16)] * (2 * DIL_GROUPS),
        compiler_params=_cparams(("arbitrary", "arbitrary", "arbitrary")),
        name="dil_prompt_attention",
    )(q, q, q, *([kv] * (2 * DIL_GROUPS)))


def _dil_sample_kernel(q_ref, new_ref, c0_ref, c1_ref, c2_ref, o_ref):
    q_row = q_ref[0]
    new_row = new_ref[0]
    caches = (c0_ref, c1_ref, c2_ref)
    lo = _lane_is_lo((ROWS8, LANES))
    row = lax.broadcasted_iota(jnp.int32, (ROWS8, LANES), 0)
    for p in range(DIL_GD // LANES):
        m_run = jnp.full((ROWS8, 1), NEG, F32)
        l_run = jnp.zeros((ROWS8, 1), F32)
        acc = jnp.zeros((ROWS8, LANES), F32)
        for g in range(DIL_GROUPS):
            qp = jnp.broadcast_to(q_row[:, g * DIL_GD + p * LANES:g * DIL_GD + (p + 1) * LANES], (ROWS8, LANES))
            q8 = jnp.where((row == 0) & lo, qp, jnp.where((row == 1) & ~lo, qp, 0.0))
            num, den, m_g = _attend_extra(
                q8, caches[g][0, :, p * LANES:(p + 1) * LANES], caches[g][0, :, DIL_GD + p * LANES:DIL_GD + (p + 1) * LANES],
                None,
                new_row[:, g * 2 * DIL_GD + p * LANES:g * 2 * DIL_GD + (p + 1) * LANES],
                new_row[:, (g * 2 + 1) * DIL_GD + p * LANES:(g * 2 + 1) * DIL_GD + (p + 1) * LANES])
            m_new = jnp.maximum(m_run, m_g)
            a_old = jnp.exp(m_run - m_new)
            a_g = jnp.exp(m_g - m_new)
            l_run = a_old * l_run + a_g * den
            acc = a_old * acc + a_g * num
            m_run = m_new
        o = acc / l_run
        o_ref[0, :, p * LANES:(p + 1) * LANES] = jnp.where(lo[0:1], o[0:1], o[1:2])


def dil_sample_attention(q_s, kv_new, caches):
    n_b = q_s.shape[0]
    views = []
    for c, (w, r) in zip(caches, DIL_PAIRS):
        assert c.shape[1] == w
        views.append(c.reshape(n_b, w // r, r * 2 * DIL_GD))
    cspec = lambda w, r: pl.BlockSpec((1, w // r, 2 * DIL_GD), lambda b: (b, 0, 0))
    out = pl.pallas_call(
        _dil_sample_kernel,
        grid=(n_b,),
        in_specs=[pl.BlockSpec((1, 1, DIL_QD), lambda b: (b, 0, 0)),
                  pl.BlockSpec((1, 1, 2 * DIL_QD), lambda b: (b, 0, 0))] + [cspec(w, r) for w, r in DIL_PAIRS],
        out_specs=pl.BlockSpec((1, 1, DIL_GD), lambda b: (b, 0, 0)),
        out_shape=jax.ShapeDtypeStruct((n_b, 1, DIL_GD), F32),
        compiler_params=_cparams(("parallel",)),
        name="dil_sample_attention",
    )(q_s.reshape(n_b, 1, DIL_QD), kv_new.reshape(n_b, 1, 2 * DIL_QD), *views)
    return out.reshape(n_b, DIL_GD)


def kernel(x_prompt, x_sample, cache_nsa_kv, cache_nsa_win, state_lru_conv, state_lru_h, cache_dil_win0, cache_dil_win1, cache_dil_win2, page_table, norm_mix, norm_ffn, nsa_w_in, nsa_q_norm, nsa_k_norm, nsa_cmp_w1, nsa_cmp_b1, nsa_cmp_w2, nsa_cmp_b2, nsa_w_out, lru_w_in, lru_conv_w, lru_conv_b, lru_gate_a_w, lru_gate_a_b, lru_gate_x_w, lru_gate_x_b, lru_lambda, lru_w_out, dil_w_in, dil_q_norm, dil_k_norm, dil_w_out, moe_router_group_w, moe_router_group_b, moe_router_expert_w, moe_router_expert_b, moe_w_in, moe_w_out):
    depth = norm_mix.shape[0]
    n_pr, seq, d = x_prompt.shape
    n_dec, dec_seq, _ = x_sample.shape
    assert dec_seq == 1
    past_len = page_table.shape[1] * PAGE_SIZE
    dil_caches = (cache_dil_win0, cache_dil_win1, cache_dil_win2)
    n_p = n_pr * seq
    xp = x_prompt.reshape(n_p, d)
    xs = x_sample.reshape(n_dec, d)

    kv_p, kv_s, nw_p, nw_s = [], [], [], []
    cv_p, cv_s, hh_p, hh_s = [], [], [], []
    dw_p, dw_s = ([], [], []), ([], [], [])
    for layer in range(depth):
        j = layer // N_MIXERS
        if layer % N_MIXERS == 0:
            phi = (nsa_k_norm[j][0], nsa_cmp_w1[j], nsa_cmp_b1[j], nsa_cmp_w2[j], nsa_cmp_b2[j])
            q, gates, rows, win = nsa_proj(xp, norm_mix[layer], nsa_w_in[j], nsa_q_norm[j], nsa_k_norm[j])
            cmp = cmp_prompt(rows, n_pr, seq, *phi)
            op = nsa_prompt_attention(q, gates, cmp, rows, win, n_pr, seq)
            xp = matmul_res(op, nsa_w_out[j], xp)
            q_s, g_s, rows_s, win_s = nsa_proj(xs, norm_mix[layer], nsa_w_in[j], nsa_q_norm[j], nsa_k_norm[j], hp=True)
            cmp_s = cmp_sample(cache_nsa_kv[j], page_table, rows_s, *phi)
            o_cmp, flags = sel_sample(q_s, cmp_s, past_len)
            os_ = attn_sample(q_s, g_s, o_cmp, flags, rows_s, win_s, cache_nsa_win[j], cache_nsa_kv[j], page_table)
            xs = matmul_res(os_, nsa_w_out[j], xs, hp=True)
            kv_p.append(rows.reshape(n_pr, seq, 4, NSA_KV_HEADS, HEAD_DIM))
            kv_s.append(rows_s.reshape(n_dec, dec_seq, 4, NSA_KV_HEADS, HEAD_DIM))
            nw_p.append(win.reshape(n_pr, seq, 2, NSA_KV_HEADS, HEAD_DIM)[:, seq - min(NSA_WINDOW, seq):])
            nw_s.append(jnp.concatenate([cache_nsa_win[j][:, dec_seq:],
                                         win_s.reshape(n_dec, dec_seq, 2, NSA_KV_HEADS, HEAD_DIM)], axis=1))
        elif layer % N_MIXERS == 1:
            lp = (lru_conv_w[j], lru_conv_b[j], lru_gate_a_w[j], lru_gate_a_b[j],
                  lru_gate_x_w[j], lru_gate_x_b[j], lru_lambda[j])
            xy_p = norm_proj(xp, norm_mix[layer], lru_w_in[j]).reshape(n_pr, seq, 2 * D_RNN)
            op, h_p = lru_prompt(xy_p, *lp)
            xp = matmul_res(op.reshape(n_p, D_RNN), lru_w_out[j], xp)
            xy_s = norm_proj(xs, norm_mix[layer], lru_w_in[j], hp=True)
            os_, h_s = lru_step(xy_s, jnp.swapaxes(state_lru_conv[j], 0, 1), state_lru_h[j], *lp)
            xs = matmul_res(os_, lru_w_out[j], xs, hp=True)
            cv_p.append(xy_p[:, seq - (CONV_W - 1):, :D_RNN])
            cv_s.append(jnp.concatenate([state_lru_conv[j], xy_s[:, None, :D_RNN]], axis=1)[:, -(CONV_W - 1):])
            hh_p.append(h_p[:, 0])
            hh_s.append(h_s)
        else:
            q, kv = dil_proj(xp, norm_mix[layer], dil_w_in[j], dil_q_norm[j], dil_k_norm[j])
            op = dil_prompt_attention(q, kv, n_pr, seq)
            xp = matmul_res(op, dil_w_out[j], xp)
            q_s, kv_s_new = dil_proj(xs, norm_mix[layer], dil_w_in[j], dil_q_norm[j], dil_k_norm[j], hp=True)
            os_ = dil_sample_attention(q_s, kv_s_new, [c[j] for c in dil_caches])
            xs = matmul_res(os_, dil_w_out[j], xs, hp=True)
            kvp = kv.reshape(n_pr, seq, DIL_GROUPS, 2, DIL_HEADS, HEAD_DIM)
            kvs = kv_s_new.reshape(n_dec, dec_seq, DIL_GROUPS, 2, DIL_HEADS, HEAD_DIM)
            for g, (w, _) in enumerate(DIL_PAIRS):
                dw_p[g].append(kvp[:, seq - min(w, seq):, g])
                dw_s[g].append(jnp.concatenate([dil_caches[g][j][:, dec_seq:], kvs[:, :, g]], axis=1))
        moe = (norm_ffn[layer], moe_router_group_w[layer], moe_router_group_b[layer],
               moe_router_expert_w[layer], moe_router_expert_b[layer], moe_w_in[layer], moe_w_out[layer])
        xp = hier_moe_res(xp, *moe)
        xs = hier_moe_res(xs, *moe, hp=True)
    return (xp.reshape(n_pr, seq, d), xs.reshape(n_dec, dec_seq, d),
            jnp.stack(kv_p), jnp.stack(kv_s), jnp.stack(nw_p), jnp.stack(nw_s),
            jnp.stack(cv_p), jnp.stack(cv_s), jnp.stack(hh_p), jnp.stack(hh_s),
            jnp.stack(dw_p[0]), jnp.stack(dw_s[0]), jnp.stack(dw_p[1]), jnp.stack(dw_s[1]),
            jnp.stack(dw_p[2]), jnp.stack(dw_s[2]))
```

```python
import functools

import numpy as np
import jax
import jax.numpy as jnp
from jax import lax
from jax.experimental import pallas as pl
from jax.experimental.pallas import tpu as pltpu

F32 = jnp.float32
BF16 = jnp.bfloat16

D_MODEL = 1024
N_MIXERS = 3
EPS = 1e-6
NEG = -1e30
BIG = 1e9
TINY = 1e-30
HEAD_DIM = 64

NSA_HEADS = 16
NSA_KV_HEADS = 4
NSA_GROUP = NSA_HEADS // NSA_KV_HEADS
CMP_BLOCK = 32
CMP_STRIDE = 16
SEL_BLOCK = 64
SEL_TOP = 16
NSA_WINDOW = 512
PHI_HIDDEN = 128

D_RNN = D_MODEL
CONV_W = 4
LRU_BLOCKS = 4
LRU_BLOCK_W = D_RNN // LRU_BLOCKS
LRU_C = 8.0

DIL_PAIRS = ((128, 1), (512, 4), (2048, 16))
DIL_GROUPS = 3
DIL_HEADS = 8

N_EXPERT_GROUPS = 4
EXPERTS_PER_GROUP = 4
N_EXPERTS = N_EXPERT_GROUPS * EXPERTS_PER_GROUP
D_EXPERT = 512

PAGE_SIZE = 128

LANES = 128
SUBLANES = 8
VMEM_LIMIT = 56 * 1024 * 1024

TOKEN_TILE = 256
MOE_TILE = 512
LRU_CHUNK = 256


def _cparams(sem):
    return pltpu.CompilerParams(dimension_semantics=sem, vmem_limit_bytes=VMEM_LIMIT)


def _rms(x, g):
    ms = jnp.mean(x * x, axis=-1, keepdims=True)
    return x * lax.rsqrt(ms + EPS) * g


def _split2(x):
    hi = x.astype(BF16)
    return hi, (x - hi.astype(F32)).astype(BF16)


def _split2_masked(x):
    hi = lax.bitcast_convert_type(lax.bitcast_convert_type(x, jnp.uint32) & jnp.uint32(0xFFFF0000), F32)
    return hi.astype(BF16), (x - hi).astype(BF16)


def _split3(x):
    h1 = x.astype(BF16)
    r1 = x - h1.astype(F32)
    h2 = r1.astype(BF16)
    h3 = (r1 - h2.astype(F32)).astype(BF16)
    return h1, h2, h3


def _dot_nn(a, b):
    return jnp.dot(a, b, preferred_element_type=F32)


def _dot_nt(a, b):
    return lax.dot_general(a, b, (((1,), (1,)), ((), ())), preferred_element_type=F32)


def _dot3_parts(a, b, dot):
    return dot(a[0], b[0]) + (dot(a[0], b[1]) + dot(a[1], b[0]))


def _mm(a, w, hp):
    if hp:
        return _dot3_parts(_split2(a.astype(F32)), _split2(w.astype(F32)), _dot_nn)
    return _dot_nn(a.astype(BF16), w.astype(BF16))


def _mm_nt(a, b, hp):
    if hp:
        return _dot3_parts(_split2(a.astype(F32)), _split2(b.astype(F32)), _dot_nt)
    return _dot_nt(a.astype(BF16), b.astype(BF16))


def _wdtype(hp):
    return F32 if hp else BF16


def _tile(m):
    return min(TOKEN_TILE, m)


def _norm_proj_kernel(x_ref, g_ref, w_ref, o_ref, *, hp):
    o_ref[...] = _mm(_rms(x_ref[...], g_ref[...]), w_ref[...], hp)


def norm_proj(x, g, w, hp=False):
    m, d = x.shape
    n = w.shape[1]
    tm = _tile(m)
    return pl.pallas_call(
        functools.partial(_norm_proj_kernel, hp=hp),
        grid=(m // tm,),
        in_specs=[pl.BlockSpec((tm, d), lambda i: (i, 0)),
                  pl.BlockSpec((1, d), lambda i: (0, 0)),
                  pl.BlockSpec((d, n), lambda i: (0, 0))],
        out_specs=pl.BlockSpec((tm, n), lambda i: (i, 0)),
        out_shape=jax.ShapeDtypeStruct((m, n), F32),
        compiler_params=_cparams(("parallel",)),
        name="norm_proj",
    )(x, g.reshape(1, d), w.astype(_wdtype(hp)))


def _matmul_res_kernel(a_ref, w_ref, r_ref, o_ref, *, hp):
    o_ref[...] = r_ref[...] + _mm(a_ref[...], w_ref[...], hp)


def matmul_res(a, w, res, hp=False):
    m, k = a.shape
    n = w.shape[1]
    tm = _tile(m)
    return pl.pallas_call(
        functools.partial(_matmul_res_kernel, hp=hp),
        grid=(m // tm,),
        in_specs=[pl.BlockSpec((tm, k), lambda i: (i, 0)),
                  pl.BlockSpec((k, n), lambda i: (0, 0)),
                  pl.BlockSpec((tm, n), lambda i: (i, 0))],
        out_specs=pl.BlockSpec((tm, n), lambda i: (i, 0)),
        out_shape=jax.ShapeDtypeStruct((m, n), F32),
        compiler_params=_cparams(("parallel",)),
        name="matmul_res",
    )(a, w.astype(_wdtype(hp)), res)


ROUTER_LANES = LANES


def _router_kernel(x_ref, g_ref, w_ref, b_ref, xn_ref, comb_ref):
    xn = _rms(x_ref[...], g_ref[...])
    xn_ref[...] = xn.astype(xn_ref.dtype)
    z = _mm(xn, w_ref[...], True) + b_ref[...]
    lane = lax.broadcasted_iota(jnp.int32, z.shape, 1)
    is_g = lane < N_EXPERT_GROUPS
    ninf = jnp.float32(-jnp.inf)
    far = jnp.int32(4 * LANES)
    zg = jnp.where(is_g, z, ninf)
    mg = jnp.max(zg, axis=-1, keepdims=True)
    g_top = jnp.min(jnp.where(is_g & (z == mg), lane, far), axis=-1, keepdims=True)
    den = jnp.sum(jnp.where(is_g, jnp.exp(zg - mg), 0.0), axis=-1, keepdims=True)
    g_gate = 1.0 / den
    lo = N_EXPERT_GROUPS + EXPERTS_PER_GROUP * g_top
    sel = (lane >= lo) & (lane < lo + EXPERTS_PER_GROUP)
    z1 = jnp.where(sel, z, ninf)
    v1 = jnp.max(z1, axis=-1, keepdims=True)
    i1 = jnp.min(jnp.where(sel & (z == v1), lane, far), axis=-1, keepdims=True)
    sel2 = sel & (lane != i1)
    z2 = jnp.where(sel2, z, ninf)
    v2 = jnp.max(z2, axis=-1, keepdims=True)
    i2 = jnp.min(jnp.where(sel2 & (z == v2), lane, far), axis=-1, keepdims=True)
    e2 = jnp.exp(v2 - v1)
    s = 1.0 + e2
    comb_ref[...] = jnp.where(lane == i1, g_gate * (1.0 / s),
                              jnp.where(lane == i2, g_gate * (e2 / s), 0.0))


def moe_router(x, g, w, b, hp):
    m, d = x.shape
    tm = _tile(m)
    return pl.pallas_call(
        _router_kernel,
        grid=(m // tm,),
        in_specs=[pl.BlockSpec((tm, d), lambda i: (i, 0)),
                  pl.BlockSpec((1, d), lambda i: (0, 0)),
                  pl.BlockSpec((d, ROUTER_LANES), lambda i: (0, 0)),
                  pl.BlockSpec((1, ROUTER_LANES), lambda i: (0, 0))],
        out_specs=[pl.BlockSpec((tm, d), lambda i: (i, 0)),
                   pl.BlockSpec((tm, ROUTER_LANES), lambda i: (i, 0))],
        out_shape=[jax.ShapeDtypeStruct((m, d), _wdtype(hp)),
                   jax.ShapeDtypeStruct((m, ROUTER_LANES), F32)],
        compiler_params=_cparams(("parallel",)),
        name="moe_router",
    )(x, g.reshape(1, d), w, b)


def _moe_dense_kernel(xn_ref, comb_ref, res_ref, win_ref, wout_ref, o_ref, acc_ref, *, hp):
    e = pl.program_id(1)

    @pl.when(e == 0)
    def _():
        acc_ref[...] = jnp.zeros_like(acc_ref)

    gu = _mm(xn_ref[...], win_ref[0], hp)
    gate, up = gu[:, :D_EXPERT], gu[:, D_EXPERT:]
    act = (gate * jax.nn.sigmoid(gate)) * up
    y = _mm(act, wout_ref[0], hp)
    comb = comb_ref[...]
    lane = lax.broadcasted_iota(jnp.int32, comb.shape, 1)
    ce = jnp.sum(jnp.where(lane == N_EXPERT_GROUPS + e, comb, 0.0), axis=-1, keepdims=True)
    acc_ref[...] += ce * y

    @pl.when(e == N_EXPERTS - 1)
    def _():
        o_ref[...] = res_ref[...] + acc_ref[...]


def moe_dense(xn, comb, res, w_in, w_out, hp):
    m, d = xn.shape
    tm = min(MOE_TILE, m)
    return pl.pallas_call(
        functools.partial(_moe_dense_kernel, hp=hp),
        grid=(m // tm, N_EXPERTS),
        in_specs=[pl.BlockSpec((tm, d), lambda i, e: (i, 0)),
                  pl.BlockSpec((tm, ROUTER_LANES), lambda i, e: (i, 0)),
                  pl.BlockSpec((tm, d), lambda i, e: (i, 0)),
                  pl.BlockSpec((1, d, 2 * D_EXPERT), lambda i, e: (e, 0, 0)),
                  pl.BlockSpec((1, D_EXPERT, d), lambda i, e: (e, 0, 0))],
        out_specs=pl.BlockSpec((tm, d), lambda i, e: (i, 0)),
        out_shape=jax.ShapeDtypeStruct((m, d), F32),
        scratch_shapes=[pltpu.VMEM((tm, d), F32)],
        compiler_params=_cparams(("parallel", "arbitrary")),
        name="moe_dense",
    )(xn, comb, res, w_in, w_out)


def hier_moe_res(x, g, rg_w, rg_b, re_w, re_b, w_in, w_out, hp=False):
    d = x.shape[1]
    n_log = N_EXPERT_GROUPS + N_EXPERTS
    w = jnp.zeros((d, ROUTER_LANES), F32).at[:, :N_EXPERT_GROUPS].set(rg_w).at[:, N_EXPERT_GROUPS:n_log].set(re_w)
    b = jnp.zeros((1, ROUTER_LANES), F32).at[0, :N_EXPERT_GROUPS].set(rg_b).at[0, N_EXPERT_GROUPS:n_log].set(re_b)
    xn, comb = moe_router(x, g, w, b, hp)
    return moe_dense(xn, comb, x, w_in, w_out, hp)


def _expm1(z):
    u = jnp.exp(z)
    d = u - 1.0
    comp = d * z / jnp.log(u)
    return jnp.where(d == 0.0, z, jnp.where(z < -1.0, d, comp))


def _gelu(x):
    return jax.nn.gelu(x)


def _lru_gates(xc, ga_w_ref, ga_b, gx_w_ref, gx_b, lam, hp):
    rs, is_ = [], []
    for n in range(LRU_BLOCKS):
        xg = xc[:, n * LRU_BLOCK_W:(n + 1) * LRU_BLOCK_W]
        rs.append(_mm(xg, ga_w_ref[n], hp))
        is_.append(_mm(xg, gx_w_ref[n], hp))
    r = jax.nn.sigmoid(jnp.concatenate(rs, axis=-1) + ga_b)
    i = jax.nn.sigmoid(jnp.concatenate(is_, axis=-1) + gx_b)
    softplus_neg_lam = jnp.maximum(-lam, 0.0) + jnp.log1p(jnp.exp(-jnp.abs(lam)))
    log_a = -LRU_C * r * softplus_neg_lam
    a = jnp.exp(log_a)
    b = jnp.sqrt(-_expm1(2.0 * log_a)) * (i * xc)
    return a, b


def _lru_prompt_kernel(xy_ref, cw_ref, cb_ref, ga_w_ref, ga_b_ref, gx_w_ref, gx_b_ref, lam_ref,
                       o_ref, hlast_ref, xext_ref, a_ref, b_ref, hs_ref, h_ref):
    c = pl.program_id(1)
    tc = LRU_CHUNK

    @pl.when(c == 0)
    def _():
        xext_ref[0:SUBLANES, :] = jnp.zeros((SUBLANES, D_RNN), F32)
        h_ref[...] = jnp.zeros_like(h_ref)

    xb = xy_ref[0, :, :D_RNN]
    xext_ref[SUBLANES:, :] = xb
    xc = cb_ref[...]
    for k in range(CONV_W):
        off = SUBLANES - (CONV_W - 1) + k
        xc = xc + xext_ref[off:off + tc, :] * cw_ref[k:k + 1, :]
    xext_ref[0:SUBLANES, :] = xb[tc - SUBLANES:, :]
    a, b = _lru_gates(xc, ga_w_ref, ga_b_ref[...], gx_w_ref, gx_b_ref[...], lam_ref[...], False)
    a_ref[...] = a
    b_ref[...] = b

    def step(t, h):
        h = a_ref[pl.ds(t, 1), :] * h + b_ref[pl.ds(t, 1), :]
        hs_ref[pl.ds(t, 1), :] = h
        return h

    h = lax.fori_loop(0, tc, step, h_ref[...], unroll=8)
    h_ref[...] = h
    hlast_ref[0] = h
    o_ref[0] = (hs_ref[...] * _gelu(xy_ref[0, :, D_RNN:])).astype(o_ref.dtype)


def lru_prompt(xy, cw, cb, ga_w, ga_b, gx_w, gx_b, lam):
    bsz, t, _ = xy.shape
    tc = LRU_CHUNK
    row = lambda v: v.reshape(1, D_RNN)
    full2 = lambda shp: pl.BlockSpec(shp, lambda b, c: (0, 0))
    full3 = lambda shp: pl.BlockSpec(shp, lambda b, c: (0, 0, 0))
    return pl.pallas_call(
        _lru_prompt_kernel,
        grid=(bsz, t // tc),
        in_specs=[pl.BlockSpec((1, tc, 2 * D_RNN), lambda b, c: (b, c, 0)),
                  full2((CONV_W, D_RNN)), full2((1, D_RNN)),
                  full3((LRU_BLOCKS, LRU_BLOCK_W, LRU_BLOCK_W)), full2((1, D_RNN)),
                  full3((LRU_BLOCKS, LRU_BLOCK_W, LRU_BLOCK_W)), full2((1, D_RNN)),
                  full2((1, D_RNN))],
        out_specs=[pl.BlockSpec((1, tc, D_RNN), lambda b, c: (b, c, 0)),
                   pl.BlockSpec((1, 1, D_RNN), lambda b, c: (b, 0, 0))],
        out_shape=[jax.ShapeDtypeStruct((bsz, t, D_RNN), BF16),
                   jax.ShapeDtypeStruct((bsz, 1, D_RNN), F32)],
        scratch_shapes=[pltpu.VMEM((tc + SUBLANES, D_RNN), F32),
                        pltpu.VMEM((tc, D_RNN), F32),
                        pltpu.VMEM((tc, D_RNN), F32),
                        pltpu.VMEM((tc, D_RNN), F32),
                        pltpu.VMEM((1, D_RNN), F32)],
        compiler_params=_cparams(("parallel", "arbitrary")),
        name="lru_prompt",
    )(xy, cw, row(cb), ga_w.astype(BF16), row(ga_b), gx_w.astype(BF16), row(gx_b), row(lam))


def _lru_step_kernel(xy_ref, cbuf_ref, h0_ref, cw_ref, cb_ref, ga_w_ref, ga_b_ref, gx_w_ref, gx_b_ref, lam_ref,
                     o_ref, h_ref):
    xb = xy_ref[:, :D_RNN]
    xc = cb_ref[...]
    for k in range(CONV_W - 1):
        xc = xc + cbuf_ref[k] * cw_ref[k:k + 1, :]
    xc = xc + xb * cw_ref[CONV_W - 1:CONV_W, :]
    a, b = _lru_gates(xc, ga_w_ref, ga_b_ref[...], gx_w_ref, gx_b_ref[...], lam_ref[...], True)
    h = a * h0_ref[...] + b
    h_ref[...] = h
    o_ref[...] = h * _gelu(xy_ref[:, D_RNN:])


def lru_step(xy, cbuf, h0, cw, cb, ga_w, ga_b, gx_w, gx_b, lam):
    bsz = xy.shape[0]
    row = lambda v: v.reshape(1, D_RNN)
    return pl.pallas_call(
        _lru_step_kernel,
        out_shape=[jax.ShapeDtypeStruct((bsz, D_RNN), F32),
                   jax.ShapeDtypeStruct((bsz, D_RNN), F32)],
        compiler_params=pltpu.CompilerParams(vmem_limit_bytes=VMEM_LIMIT),
        name="lru_step",
    )(xy, cbuf, h0, cw, row(cb), ga_w, row(ga_b), gx_w, row(gx_b), row(lam))


HALF = HEAD_DIM
ROWS8 = SUBLANES


def _lane_is_lo(shape):
    return lax.broadcasted_iota(jnp.int32, shape, len(shape) - 1) < HALF


def _pair_head_norm(y, gamma):
    sq = y * y
    lo = _lane_is_lo(y.shape)
    ss_lo = jnp.sum(jnp.where(lo, sq, 0.0), axis=-1, keepdims=True)
    ss_hi = jnp.sum(jnp.where(lo, 0.0, sq), axis=-1, keepdims=True)
    ms = jnp.where(lo, ss_lo, ss_hi) * (1.0 / HEAD_DIM)
    return y * lax.rsqrt(ms + EPS) * gamma


def _roll_half(x):
    return pltpu.roll(x, HALF, axis=x.ndim - 1)


def _online_update(carry, s, vblk):
    m_old, l_old, acc_old = carry
    m_new = jnp.maximum(m_old, jnp.max(s, axis=-1, keepdims=True))
    alpha = jnp.exp(m_old - m_new)
    e = jnp.exp(s - m_new)
    l_new = alpha * l_old + jnp.sum(e, axis=-1, keepdims=True)
    acc_new = alpha * acc_old + _dot_nn(e.astype(BF16), vblk)
    return m_new, l_new, acc_new


def _attend_extra(q8, keys, vals, kmask, k_new, v_new):
    q_parts = _split2(q8)
    s = _dot3_parts(q_parts, _split2(keys), _dot_nt)
    if kmask is not None:
        s = jnp.where(kmask, s, NEG)
    s_new = jnp.sum(q8 * k_new, axis=-1, keepdims=True)
    m = jnp.maximum(jnp.max(s, axis=-1, keepdims=True), s_new)
    e = jnp.exp(s - m)
    e_new = jnp.exp(s_new - m)
    den = jnp.sum(e, axis=-1, keepdims=True) + e_new
    num = _dot3_parts(_split2(e), _split2(vals), _dot_nn) + e_new * v_new
    return num, den, m


NSA_QD = NSA_HEADS * HEAD_DIM
NSA_KVD = NSA_KV_HEADS * HEAD_DIM
NSA_GATE_LANES = LANES
NSA_PROJ_COLS = NSA_QD + 6 * NSA_KVD + NSA_GATE_LANES


def _nsa_proj_kernel(x_ref, g_ref, w_ref, qn_ref, kn_ref, q_ref, gate_ref, rows_ref, win_ref, *, hp):
    y = _mm(_rms(x_ref[...], g_ref[...]), w_ref[...], hp)
    qn = qn_ref[...]
    for p in range(NSA_QD // LANES):
        blk = y[:, p * LANES:(p + 1) * LANES]
        q_ref[:, p * LANES:(p + 1) * LANES] = (_pair_head_norm(blk, qn) * (HEAD_DIM ** -0.5)).astype(q_ref.dtype)
    kv0 = NSA_QD
    for c in range(6):
        for p in range(NSA_KVD // LANES):
            lo = kv0 + c * NSA_KVD + p * LANES
            blk = y[:, lo:lo + LANES]
            if c == 2:
                blk = _pair_head_norm(blk, kn_ref[0:1, :])
            elif c == 4:
                blk = _pair_head_norm(blk, kn_ref[1:2, :])
            if c < 4:
                rows_ref[:, c * NSA_KVD + p * LANES:c * NSA_KVD + (p + 1) * LANES] = blk
            else:
                win_ref[:, (c - 4) * NSA_KVD + p * LANES:(c - 4) * NSA_KVD + (p + 1) * LANES] = blk
    gate_ref[...] = jax.nn.sigmoid(y[:, kv0 + 6 * NSA_KVD:])


def nsa_proj(x, g, w_in, q_norm, k_norm, hp=False):
    m, d = x.shape
    n_gate = 3 * NSA_HEADS
    w = jnp.concatenate([w_in[:, :NSA_QD], w_in[:, NSA_QD + n_gate:], w_in[:, NSA_QD:NSA_QD + n_gate],
                         jnp.zeros((d, NSA_GATE_LANES - n_gate), w_in.dtype)], axis=1).astype(_wdtype(hp))
    qn = jnp.tile(q_norm, 2).reshape(1, LANES)
    kn = jnp.stack([jnp.tile(k_norm[1], 2), jnp.tile(k_norm[2], 2)])
    tm = _tile(m)
    return pl.pallas_call(
        functools.partial(_nsa_proj_kernel, hp=hp),
        grid=(m // tm,),
        in_specs=[pl.BlockSpec((tm, d), lambda i: (i, 0)),
                  pl.BlockSpec((1, d), lambda i: (0, 0)),
                  pl.BlockSpec((d, NSA_PROJ_COLS), lambda i: (0, 0)),
                  pl.BlockSpec((1, LANES), lambda i: (0, 0)),
                  pl.BlockSpec((2, LANES), lambda i: (0, 0))],
        out_specs=[pl.BlockSpec((tm, NSA_QD), lambda i: (i, 0)),
                   pl.BlockSpec((tm, NSA_GATE_LANES), lambda i: (i, 0)),
                   pl.BlockSpec((tm, 4 * NSA_KVD), lambda i: (i, 0)),
                   pl.BlockSpec((tm, 2 * NSA_KVD), lambda i: (i, 0))],
        out_shape=[jax.ShapeDtypeStruct((m, NSA_QD), _wdtype(hp)),
                   jax.ShapeDtypeStruct((m, NSA_GATE_LANES), F32),
                   jax.ShapeDtypeStruct((m, 4 * NSA_KVD), F32),
                   jax.ShapeDtypeStruct((m, 2 * NSA_KVD), F32)],
        compiler_params=_cparams(("parallel",)),
        name="nsa_proj",
    )(x, g.reshape(1, d), w, qn, kn)


def _cmp_weights(w1, b1, w2, b2):
    eye = jnp.eye(NSA_KV_HEADS, dtype=w1.dtype)
    lead_tail = jnp.concatenate([w1[:, :CMP_STRIDE], w1[:, CMP_STRIDE:]], axis=-1)
    w1bd = jnp.einsum('csdh,kj->cskdjh', lead_tail, eye).reshape(2, CMP_STRIDE, NSA_KVD, NSA_KV_HEADS * 2 * PHI_HIDDEN)
    w2bd = jnp.einsum('chd,kj->ckhjd', w2, eye).reshape(2, NSA_KV_HEADS * PHI_HIDDEN, NSA_KVD)
    b1t = jnp.tile(b1, (1, NSA_KV_HEADS)).reshape(2, 1, NSA_KV_HEADS * PHI_HIDDEN)
    b2t = jnp.tile(b2, (1, NSA_KV_HEADS)).reshape(2, 1, NSA_KVD)
    return w1bd, b1t, w2bd, b2t


def _compress_halves(first_layer, b1, second_layer, b2, knorm, is_key, extra_tail=None):
    acc = first_layer()
    n_half = acc.shape[0]
    hids = []
    for k in range(NSA_KV_HEADS):
        lead = acc[:, k * 2 * PHI_HIDDEN:k * 2 * PHI_HIDDEN + PHI_HIDDEN]
        tail = acc[:, k * 2 * PHI_HIDDEN + PHI_HIDDEN:(k + 1) * 2 * PHI_HIDDEN]
        nxt = pltpu.roll(tail, n_half - 1, axis=0)
        if extra_tail is not None:
            row = lax.broadcasted_iota(jnp.int32, nxt.shape, 0)
            nxt = jnp.where(row == n_half - 1, extra_tail[:, k * PHI_HIDDEN:(k + 1) * PHI_HIDDEN], nxt)
        hids.append(lead + nxt)
    hid = _gelu(jnp.concatenate(hids, axis=-1) + b1)
    out = second_layer(hid) + b2
    normed = jnp.concatenate([_pair_head_norm(out[:, p * LANES:(p + 1) * LANES], knorm)
                              for p in range(NSA_KVD // LANES)], axis=-1)
    return jnp.where(is_key, normed, out)


def _cmp_prompt_kernel(x_ref, w1_ref, b1_ref, w2_ref, b2_ref, kn_ref, o_ref):
    c = pl.program_id(1)

    def first_layer():
        acc = None
        for s in range(CMP_STRIDE):
            part = _dot_nn(x_ref[:, s, :].astype(BF16), w1_ref[0, s])
            acc = part if acc is None else acc + part
        return acc

    out = _compress_halves(first_layer, b1_ref[0], lambda hid: _dot_nn(hid.astype(BF16), w2_ref[0]), b2_ref[0],
                           kn_ref[...], c == 0)
    o_ref[0, 0] = out.astype(o_ref.dtype)


def cmp_prompt(rows, n_b, seq, k_cmp_norm, w1, b1, w2, b2):
    n_half = seq // CMP_STRIDE
    x3 = rows.reshape(rows.shape[0] // CMP_STRIDE, CMP_STRIDE, 4 * NSA_KVD)
    w1bd, b1t, w2bd, b2t = _cmp_weights(w1, b1, w2, b2)
    kn = jnp.tile(k_cmp_norm, 2).reshape(1, LANES)
    return pl.pallas_call(
        _cmp_prompt_kernel,
        grid=(n_b, 2),
        in_specs=[pl.BlockSpec((n_half, CMP_STRIDE, NSA_KVD), lambda b, c: (b, 0, c)),
                  pl.BlockSpec((1, CMP_STRIDE, NSA_KVD, w1bd.shape[-1]), lambda b, c: (c, 0, 0, 0)),
                  pl.BlockSpec((1, 1, b1t.shape[-1]), lambda b, c: (c, 0, 0)),
                  pl.BlockSpec((1, w2bd.shape[1], NSA_KVD), lambda b, c: (c, 0, 0)),
                  pl.BlockSpec((1, 1, NSA_KVD), lambda b, c: (c, 0, 0)),
                  pl.BlockSpec((1, LANES), lambda b, c: (0, 0))],
        out_specs=pl.BlockSpec((1, 1, n_half, NSA_KVD), lambda b, c: (b, c, 0, 0)),
        out_shape=jax.ShapeDtypeStruct((n_b, 2, n_half, NSA_KVD), BF16),
        compiler_params=_cparams(("parallel", "arbitrary")),
        name="nsa_cmp_prompt",
    )(x3, w1bd.astype(BF16), b1t, w2bd.astype(BF16), b2t, kn)


NSA_TQ = 128
NSA_KC = 512
NSA_WCHUNK = NSA_WINDOW + NSA_TQ


def _sel_map(n_sel, n_cmp, rows, cols):
    ratio = SEL_BLOCK // CMP_STRIDE
    span = CMP_BLOCK // CMP_STRIDE
    idx = np.arange(n_sel)[:, None] * ratio - (span - 1) + np.arange(ratio + span - 1)[None, :]
    wgt = np.convolve(np.ones(ratio), np.ones(span))[None, :] * ((idx >= 0) & (idx < n_cmp))
    m = np.zeros((rows, cols), np.float32)
    for j in range(n_sel):
        for a in range(idx.shape[1]):
            if wgt[j, a] != 0:
                m[idx[j, a], j] += wgt[j, a]
    return m


def _topk_mask_cols(score_t, n_top):
    n = score_t.shape[0]
    row = lax.broadcasted_iota(jnp.int32, score_t.shape, 0)
    rank = jnp.zeros(score_t.shape, F32)
    for i in range(n):
        ri = score_t[i:i + 1, :]
        beats = jnp.where(ri > score_t, 1.0, jnp.where((ri == score_t) & (row > i), 1.0, 0.0))
        rank = rank + beats
    return jnp.where(rank < n_top, 1.0, 0.0)


def _nsa_prompt_kernel(q_ref, gate_ref, kc_ref, vc_ref, ks_ref, vs_ref, kw_ref, vw_ref, map_ref, exp_ref,
                       o_ref, ks_bf, vs_bf, kw_bf, vw_bf):
    kv = pl.program_id(1)
    i = pl.program_id(2)
    tq = NSA_TQ
    seq = ks_ref.shape[0]
    n_sel = seq // SEL_BLOCK

    @pl.when(i == 0)
    def _():
        ks_bf[...] = ks_ref[...].astype(BF16)
        vs_bf[...] = vs_ref[...].astype(BF16)
        kw_bf[...] = kw_ref[...].astype(BF16)
        vw_bf[...] = vw_ref[...].astype(BF16)

    hv = kv % 2
    t0 = i * tq
    t = t0 + lax.broadcasted_iota(jnp.int32, (tq, 1), 0)
    lo128 = _lane_is_lo((tq, LANES))

    qs = []
    for g in range(NSA_GROUP):
        pair = q_ref[:, (g // 2) * LANES:(g // 2 + 1) * LANES].astype(F32)
        own = jnp.where(lo128 if g % 2 == 0 else ~lo128, pair, 0.0)
        qs.append(jnp.where(hv == g % 2, own, _roll_half(own)).astype(BF16))

    gates = gate_ref[...]
    glane = lax.broadcasted_iota(jnp.int32, gates.shape, 1)

    def gate_col(g, c):
        idx = (kv * NSA_GROUP + g) * 3 + c
        return jnp.sum(jnp.where(glane == idx, gates, 0.0), axis=-1, keepdims=True)

    kc = kc_ref[0, 0]
    vc = vc_ref[0, 0]
    n_cmp_pad = kc.shape[0]
    cmp_end = lax.broadcasted_iota(jnp.int32, (tq, n_cmp_pad), 1) * CMP_STRIDE + (CMP_BLOCK - 1)
    cmask = cmp_end <= t
    o_c = []
    imp = None
    for g in range(NSA_GROUP):
        s = jnp.where(cmask, _dot_nt(qs[g], kc), NEG)
        e = jnp.where(cmask, jnp.exp(s - jnp.max(s, axis=-1, keepdims=True)), 0.0)
        p = e / jnp.maximum(jnp.sum(e, axis=-1, keepdims=True), TINY)
        o_c.append(_dot_nn(p.astype(BF16), vc))
        imp = p if imp is None else imp + p

    p_blk = None
    for term in _split3(imp):
        part = _dot_nn(term, map_ref[...])
        p_blk = part if p_blk is None else p_blk + part
    blk = lax.broadcasted_iota(jnp.int32, (tq, n_cmp_pad), 1)
    cur = t // SEL_BLOCK
    forced = (blk == 0) | (blk == cur) | (blk == cur - 1)
    score = jnp.where(blk * SEL_BLOCK > t, NEG, jnp.where(forced, BIG, p_blk))
    score_t = score.T[0:n_sel, :]
    sel_t = _topk_mask_cols(score_t, min(SEL_TOP, n_sel))
    sel_t = jnp.concatenate([sel_t, jnp.zeros((n_cmp_pad - n_sel, tq), F32)], axis=0)
    sel = sel_t.T.astype(BF16)

    kcs = NSA_KC
    n_chunks = (t0 + tq + kcs - 1) // kcs

    def chunk(c, carry):
        c0 = pl.multiple_of(c * kcs, kcs)
        kblk = ks_bf[pl.ds(c0, kcs), :]
        vblk = vs_bf[pl.ds(c0, kcs), :]
        maskf = _dot_nn(sel, exp_ref[:, pl.ds(c0, kcs)])
        kpos = c0 + lax.broadcasted_iota(jnp.int32, (tq, kcs), 1)
        valid = jnp.where(kpos <= t, maskf, 0.0) > 0.5
        return tuple(_online_update(carry[g], jnp.where(valid, _dot_nt(qs[g], kblk), NEG), vblk)
                     for g in range(NSA_GROUP))

    init = tuple((jnp.full((tq, 1), NEG, F32), jnp.zeros((tq, 1), F32), jnp.zeros((tq, LANES), F32))
                 for _ in range(NSA_GROUP))
    fin = lax.fori_loop(0, n_chunks, chunk, init)

    ws = pl.multiple_of(jnp.maximum(t0 - NSA_WINDOW, 0), tq)
    kwin = kw_bf[pl.ds(ws, NSA_WCHUNK), :]
    vwin = vw_bf[pl.ds(ws, NSA_WCHUNK), :]
    dist = t - (ws + lax.broadcasted_iota(jnp.int32, (tq, NSA_WCHUNK), 1))
    wmask = (dist >= 0) & (dist < NSA_WINDOW)

    outs = []
    for g in range(NSA_GROUP):
        _, l_s, acc_s = fin[g]
        o_s = acc_s / l_s
        s = jnp.where(wmask, _dot_nt(qs[g], kwin), NEG)
        e = jnp.where(wmask, jnp.exp(s - jnp.max(s, axis=-1, keepdims=True)), 0.0)
        p = e / jnp.maximum(jnp.sum(e, axis=-1, keepdims=True), TINY)
        o_w = _dot_nn(p.astype(BF16), vwin)
        outs.append(gate_col(g, 0) * o_c[g] + gate_col(g, 1) * o_s + gate_col(g, 2) * o_w)

    for j in range(NSA_GROUP // 2):
        a = jnp.where(hv == 0, outs[2 * j], _roll_half(outs[2 * j]))
        b = jnp.where(hv == 1, outs[2 * j + 1], _roll_half(outs[2 * j + 1]))
        o_ref[:, j * LANES:(j + 1) * LANES] = jnp.where(lo128, a, b).astype(o_ref.dtype)


def nsa_prompt_attention(q, gates, cmp, rows, win, n_b, seq):
    tq = NSA_TQ
    nq = seq // tq
    n_cmp_pad = seq // CMP_STRIDE
    assert n_cmp_pad == LANES and seq % NSA_KC == 0
    n_sel = seq // SEL_BLOCK
    smap = jnp.asarray(_sel_map(n_sel, n_cmp_pad - 1, n_cmp_pad, n_cmp_pad), BF16)
    expand = jnp.asarray((np.arange(n_cmp_pad)[:, None] == (np.arange(seq)[None, :] // SEL_BLOCK)), BF16)
    kvspec = lambda col0: pl.BlockSpec((seq, LANES), lambda b, k, i: (b, col0 + k // 2))
    return pl.pallas_call(
        _nsa_prompt_kernel,
        grid=(n_b, NSA_KV_HEADS, nq),
        in_specs=[pl.BlockSpec((tq, NSA_GROUP * HEAD_DIM), lambda b, k, i: (b * nq + i, k)),
                  pl.BlockSpec((tq, NSA_GATE_LANES), lambda b, k, i: (b * nq + i, 0)),
                  pl.BlockSpec((1, 1, n_cmp_pad, LANES), lambda b, k, i: (b, 0, 0, k // 2)),
                  pl.BlockSpec((1, 1, n_cmp_pad, LANES), lambda b, k, i: (b, 1, 0, k // 2)),
                  kvspec(4), kvspec(6), kvspec(0), kvspec(2),
                  pl.BlockSpec((n_cmp_pad, n_cmp_pad), lambda b, k, i: (0, 0)),
                  pl.BlockSpec((n_cmp_pad, seq), lambda b, k, i: (0, 0))],
        out_specs=pl.BlockSpec((tq, NSA_GROUP * HEAD_DIM), lambda b, k, i: (b * nq + i, k)),
        out_shape=jax.ShapeDtypeStruct((n_b * seq, NSA_QD), BF16),
        scratch_shapes=[pltpu.VMEM((seq, LANES), BF16)] * 4,
        compiler_params=_cparams(("arbitrary", "arbitrary", "arbitrary")),
        name="nsa_prompt_attention",
    )(q, gates, cmp, cmp, rows, rows, win, win, smap, expand)


HALVES_PER_PAGE = PAGE_SIZE // CMP_STRIDE
SEL_PER_PAGE = PAGE_SIZE // SEL_BLOCK
N_SEL_PAST = SEL_TOP - 1


def _stacked_first_layer(xs, w_ref, s):
    his = [x.astype(BF16).astype(F32) for x in xs]
    los_rolled = [_roll_half(x - h) for x, h in zip(xs, his)]
    is_lo_lane = _lane_is_lo(xs[0].shape)
    parts = []
    for k in range(NSA_KV_HEADS):
        ph = his[k // 2]
        plr = los_rolled[k // 2]
        if k % 2 == 0:
            col0, col1 = jnp.where(is_lo_lane, ph, plr), jnp.where(is_lo_lane, ph, 0.0)
        else:
            col0, col1 = jnp.where(is_lo_lane, plr, ph), jnp.where(is_lo_lane, 0.0, ph)
        lhs = jnp.concatenate([col0, col1], axis=-1).astype(BF16)
        parts.append(_dot_nn(lhs, w_ref[0, s, k]))
    return jnp.concatenate(parts, axis=-1)


def _cmp_sample_kernel(pt_ref, cache_ref, new_ref, w1_ref, b1_ref, w2h_ref, w2l_ref, b2_ref, kn_ref,
                       o_ref, buf, sem):
    b = pl.program_id(0)
    c = pl.program_id(1)
    n_pages = pt_ref.shape[1]
    n_half = n_pages * HALVES_PER_PAGE
    col0 = pl.multiple_of(c * NSA_KVD, NSA_KVD)

    n_pairs = NSA_KVD // LANES

    def page_copy(p, pr):
        return pltpu.make_async_copy(
            cache_ref.at[pt_ref[b, p], :, pl.ds(col0 + pr * LANES, LANES)],
            buf.at[pr, pl.ds(p * PAGE_SIZE, PAGE_SIZE)], sem)

    for p in range(n_pages):
        for pr in range(n_pairs):
            page_copy(p, pr).start()
    new_row = jnp.broadcast_to(new_ref[0], (ROWS8, NSA_KVD))
    t_all = _stacked_first_layer([new_row[:, pr * LANES:(pr + 1) * LANES] for pr in range(n_pairs)], w1_ref, 0)[0:1]
    extra = jnp.concatenate([t_all[:, k * 2 * PHI_HIDDEN + PHI_HIDDEN:(k + 1) * 2 * PHI_HIDDEN]
                             for k in range(NSA_KV_HEADS)], axis=-1)
    for p in range(n_pages):
        for pr in range(n_pairs):
            page_copy(p, pr).wait()

    def first_layer():
        acc = None
        for s in range(CMP_STRIDE):
            xs = [buf[pr, pl.ds(s, n_half, stride=CMP_STRIDE), :] for pr in range(n_pairs)]
            part = _stacked_first_layer(xs, w1_ref, s)
            acc = part if acc is None else acc + part
        return acc

    out = _compress_halves(first_layer, b1_ref[0],
                           lambda hid: _dot3_parts(_split2(hid), (w2h_ref[0], w2l_ref[0]), _dot_nn), b2_ref[0],
                           kn_ref[...], c == 0, extra_tail=extra)
    o_ref[0, 0] = out


def _stacked_cmp_weights(w1):
    lead_tail = jnp.concatenate([w1[:, :CMP_STRIDE], w1[:, CMP_STRIDE:]], axis=-1)
    w_hi, w_lo = _split2_masked(lead_tail)
    zero = jnp.zeros_like(w_hi)
    even = jnp.concatenate([w_hi, w_hi, w_lo, zero], axis=2)
    odd = jnp.concatenate([w_hi, w_hi, zero, w_lo], axis=2)
    return jnp.stack([even if k % 2 == 0 else odd for k in range(NSA_KV_HEADS)], axis=2)


def cmp_sample(cache2, page_ids, rows_new, k_cmp_norm, w1, b1, w2, b2):
    n_b, n_pages = page_ids.shape
    n_half = n_pages * HALVES_PER_PAGE
    new3 = rows_new.reshape(n_b, 1, 4 * NSA_KVD)
    _, b1t, w2bd, b2t = _cmp_weights(w1, b1, w2, b2)
    w1s = _stacked_cmp_weights(w1)
    w2h, w2l = _split2_masked(w2bd)
    kn = jnp.tile(k_cmp_norm, 2).reshape(1, LANES)
    w2spec = pl.BlockSpec((1, w2bd.shape[1], NSA_KVD), lambda b, c, pt: (c, 0, 0))
    grid_spec = pltpu.PrefetchScalarGridSpec(
        num_scalar_prefetch=1,
        grid=(n_b, 2),
        in_specs=[pl.BlockSpec(memory_space=pl.ANY),
                  pl.BlockSpec((1, 1, NSA_KVD), lambda b, c, pt: (b, 0, c)),
                  pl.BlockSpec((1,) + w1s.shape[1:], lambda b, c, pt: (c, 0, 0, 0, 0)),
                  pl.BlockSpec((1, 1, b1t.shape[-1]), lambda b, c, pt: (c, 0, 0)),
                  w2spec, w2spec,
                  pl.BlockSpec((1, 1, NSA_KVD), lambda b, c, pt: (c, 0, 0)),
                  pl.BlockSpec((1, LANES), lambda b, c, pt: (0, 0))],
        out_specs=pl.BlockSpec((1, 1, n_half, NSA_KVD), lambda b, c, pt: (b, c, 0, 0)),
        scratch_shapes=[pltpu.VMEM((NSA_KVD // LANES, n_pages * PAGE_SIZE, LANES), F32),
                        pltpu.SemaphoreType.DMA(())],
    )
    return pl.pallas_call(
        _cmp_sample_kernel,
        grid_spec=grid_spec,
        out_shape=jax.ShapeDtypeStruct((n_b, 2, n_half, NSA_KVD), F32),
        compiler_params=_cparams(("arbitrary", "arbitrary")),
        name="nsa_cmp_sample",
    )(page_ids, cache2, new3, w1s, b1t, w2h, w2l, b2t, kn)


def _group_queries(q_row, kv):
    lo = _lane_is_lo((ROWS8, LANES))
    row = lax.broadcasted_iota(jnp.int32, (ROWS8, LANES), 0)
    out = jnp.zeros((ROWS8, LANES), F32)
    for g in range(NSA_GROUP):
        h = kv * NSA_GROUP + g
        pair = jnp.broadcast_to(q_row[:, (h // 2) * LANES:(h // 2 + 1) * LANES], (ROWS8, LANES))
        own = jnp.where(lo if h % 2 == 0 else ~lo, pair, 0.0)
        own = own if h % 2 == kv % 2 else _roll_half(own)
        out = jnp.where(row == g, own, out)
    return out


def _sel_sample_kernel(q_ref, cmp_ref, map_ref, oc_ref, flag_ref, *, t_pos, n_sel):
    q_row = q_ref[0]
    n_cmp = cmp_ref.shape[2]
    pad = map_ref.shape[1]
    cmask = (lax.broadcasted_iota(jnp.int32, (ROWS8, n_cmp), 1) * CMP_STRIDE + (CMP_BLOCK - 1)) <= t_pos
    lane = lax.broadcasted_iota(jnp.int32, (1, pad), 1)
    cur = t_pos // SEL_BLOCK
    forced = (lane == 0) | (lane == cur) | (lane == cur - 1)
    eye = lax.broadcasted_iota(jnp.int32, (pad, pad), 0) == lax.broadcasted_iota(jnp.int32, (pad, pad), 1)
    before = lax.broadcasted_iota(jnp.int32, (pad, pad), 0) < lax.broadcasted_iota(jnp.int32, (pad, pad), 1)
    for kv in range(NSA_KV_HEADS):
        q8 = _group_queries(q_row, kv)
        kc = cmp_ref[0, 0, :, (kv // 2) * LANES:(kv // 2 + 1) * LANES]
        vc = cmp_ref[0, 1, :, (kv // 2) * LANES:(kv // 2 + 1) * LANES]
        s = jnp.where(cmask, _dot3_parts(_split2(q8), _split2(kc), _dot_nt), NEG)
        e = jnp.where(cmask, jnp.exp(s - jnp.max(s, axis=-1, keepdims=True)), 0.0)
        p = e / jnp.maximum(jnp.sum(e, axis=-1, keepdims=True), TINY)
        oc_ref[0, kv] = _dot3_parts(_split2(p), _split2(vc), _dot_nn)
        imp = p[0:1]
        for g in range(1, NSA_GROUP):
            imp = imp + p[g:g + 1]
        imp8 = jnp.broadcast_to(imp, (ROWS8, n_cmp))
        p_blk = None
        for term in _split3(imp8):
            part = _dot_nn(term, map_ref[...])
            p_blk = part if p_blk is None else p_blk + part
        score = jnp.where(lane * SEL_BLOCK > t_pos, NEG, jnp.where(forced, BIG, p_blk[0:1]))
        s_row = jnp.broadcast_to(score, (pad, pad))
        s_col = jnp.sum(jnp.where(eye, s_row, 0.0), axis=1, keepdims=True)
        beats = jnp.where(s_col > s_row, 1.0, jnp.where((s_col == s_row) & before, 1.0, 0.0))
        rank = jnp.sum(beats, axis=0, keepdims=True)
        chosen = (rank < min(SEL_TOP, n_sel)) & (lane < n_sel - 1)
        flag_ref[0, kv] = jnp.broadcast_to(jnp.where(chosen, 1, 0).astype(jnp.int32), (ROWS8, pad))


def sel_sample(q_s, cmp_s, t_pos):
    n_b = q_s.shape[0]
    n_cmp = cmp_s.shape[2]
    n_sel = -(-(t_pos + 1) // SEL_BLOCK)
    assert n_sel - 1 >= SEL_TOP
    pad = -(-n_sel // LANES) * LANES
    smap = jnp.asarray(_sel_map(n_sel, n_cmp, n_cmp, pad), BF16)
    kern = functools.partial(_sel_sample_kernel, t_pos=t_pos, n_sel=n_sel)
    return pl.pallas_call(
        kern,
        grid=(n_b,),
        in_specs=[pl.BlockSpec((1, 1, NSA_QD), lambda b: (b, 0, 0)),
                  pl.BlockSpec((1, 2, n_cmp, NSA_KVD), lambda b: (b, 0, 0, 0)),
                  pl.BlockSpec((n_cmp, pad), lambda b: (0, 0))],
        out_specs=[pl.BlockSpec((1, NSA_KV_HEADS, ROWS8, LANES), lambda b: (b, 0, 0, 0)),
                   pl.BlockSpec((1, NSA_KV_HEADS, ROWS8, pad), lambda b: (b, 0, 0, 0))],
        out_shape=[jax.ShapeDtypeStruct((n_b, NSA_KV_HEADS, ROWS8, LANES), F32),
                   jax.ShapeDtypeStruct((n_b, NSA_KV_HEADS, ROWS8, pad), jnp.int32)],
        compiler_params=_cparams(("parallel",)),
        name="nsa_sel_sample",
    )(q_s.reshape(n_b, 1, NSA_QD), cmp_s, smap)


def _attn_sample_kernel(pt_ref, flag_ref, q_ref, gate_ref, oc_ref, new_rows_ref, new_win_ref, wcache_ref, cache_ref,
                        o_ref, kbuf, vbuf, sem):
    b = pl.program_id(0)
    n_past_blocks = pt_ref.shape[1] * SEL_PER_PAGE

    for kv in range(NSA_KV_HEADS):
        def issue(j, slot):
            @pl.when(flag_ref[0, kv, j] != 0)
            def _():
                page = pt_ref[b, j // SEL_PER_PAGE]
                r0 = pl.multiple_of((j % SEL_PER_PAGE) * SEL_BLOCK, SEL_BLOCK)
                dst = jnp.minimum(slot, N_SEL_PAST - 1)
                pltpu.make_async_copy(cache_ref.at[page, pl.ds(r0, SEL_BLOCK), pl.ds((2 * 2 + kv // 2) * LANES, LANES)],
                                      kbuf.at[kv, dst], sem.at[kv]).start()
                pltpu.make_async_copy(cache_ref.at[page, pl.ds(r0, SEL_BLOCK), pl.ds((3 * 2 + kv // 2) * LANES, LANES)],
                                      vbuf.at[kv, dst], sem.at[kv]).start()
            return slot + jnp.where(flag_ref[0, kv, j] != 0, 1, 0)

        lax.fori_loop(0, n_past_blocks, issue, jnp.int32(0))

    q_row = q_ref[0]
    gates = jnp.broadcast_to(gate_ref[0], (ROWS8, NSA_GATE_LANES))
    grow = lax.broadcasted_iota(jnp.int32, gates.shape, 0)
    glane = lax.broadcasted_iota(jnp.int32, gates.shape, 1)
    lo = _lane_is_lo((1, LANES))
    new_rows = new_rows_ref[0]
    new_win = new_win_ref[0]
    wrow = lax.broadcasted_iota(jnp.int32, (ROWS8, wcache_ref.shape[1]), 1)

    outs = []
    for kv in range(NSA_KV_HEADS):
        pr = kv // 2
        q8 = _group_queries(q_row, kv)
        for _ in range(2 * N_SEL_PAST):
            pltpu.make_async_copy(cache_ref.at[0, pl.ds(0, SEL_BLOCK), pl.ds(0, LANES)], kbuf.at[kv, 0], sem.at[kv]).wait()
        keys = kbuf[kv].reshape(N_SEL_PAST * SEL_BLOCK, LANES)
        vals = vbuf[kv].reshape(N_SEL_PAST * SEL_BLOCK, LANES)
        num, den, _ = _attend_extra(q8, keys, vals, None,
                                    new_rows[:, (2 * 2 + pr) * LANES:(2 * 2 + pr + 1) * LANES],
                                    new_rows[:, (3 * 2 + pr) * LANES:(3 * 2 + pr + 1) * LANES])
        o_s = num / den
        num, den, _ = _attend_extra(q8, wcache_ref[0, :, pr * LANES:(pr + 1) * LANES],
                                    wcache_ref[0, :, (2 + pr) * LANES:(2 + pr + 1) * LANES], wrow >= 1,
                                    new_win[:, pr * LANES:(pr + 1) * LANES],
                                    new_win[:, (2 + pr) * LANES:(2 + pr + 1) * LANES])
        o_w = num / den

        def gate_col(c):
            idx = (kv * NSA_GROUP + grow) * 3 + c
            return jnp.sum(jnp.where((glane == idx) & (grow < NSA_GROUP), gates, 0.0), axis=-1, keepdims=True)

        outs.append(gate_col(0) * oc_ref[0, kv] + gate_col(1) * o_s + gate_col(2) * o_w)

    for kv in range(NSA_KV_HEADS):
        for jj in range(NSA_GROUP // 2):
            a = outs[kv][2 * jj:2 * jj + 1]
            bb = outs[kv][2 * jj + 1:2 * jj + 2]
            a = a if kv % 2 == 0 else _roll_half(a)
            bb = bb if kv % 2 == 1 else _roll_half(bb)
            col = (kv * NSA_GROUP + 2 * jj) // 2
            o_ref[0, :, col * LANES:(col + 1) * LANES] = jnp.where(lo, a, bb)


def attn_sample(q_s, gates_s, o_cmp, flags, rows_new, win_new, win_cache, cache2, page_ids):
    n_b = q_s.shape[0]
    n_win = win_cache.shape[1]
    assert n_win == NSA_WINDOW
    wc = win_cache.reshape(n_b, n_win, 2 * NSA_KVD)
    pad = flags.shape[-1]
    grid_spec = pltpu.PrefetchScalarGridSpec(
        num_scalar_prefetch=1,
        grid=(n_b,),
        in_specs=[pl.BlockSpec((1, NSA_KV_HEADS, pad), lambda b, pt: (b, 0, 0), memory_space=pltpu.SMEM),
                  pl.BlockSpec((1, 1, NSA_QD), lambda b, pt: (b, 0, 0)),
                  pl.BlockSpec((1, 1, NSA_GATE_LANES), lambda b, pt: (b, 0, 0)),
                  pl.BlockSpec((1, NSA_KV_HEADS, ROWS8, LANES), lambda b, pt: (b, 0, 0, 0)),
                  pl.BlockSpec((1, 1, 4 * NSA_KVD), lambda b, pt: (b, 0, 0)),
                  pl.BlockSpec((1, 1, 2 * NSA_KVD), lambda b, pt: (b, 0, 0)),
                  pl.BlockSpec((1, n_win, 2 * NSA_KVD), lambda b, pt: (b, 0, 0)),
                  pl.BlockSpec(memory_space=pl.ANY)],
        out_specs=pl.BlockSpec((1, 1, NSA_QD), lambda b, pt: (b, 0, 0)),
        scratch_shapes=[pltpu.VMEM((NSA_KV_HEADS, N_SEL_PAST, SEL_BLOCK, LANES), F32),
                        pltpu.VMEM((NSA_KV_HEADS, N_SEL_PAST, SEL_BLOCK, LANES), F32),
                        pltpu.SemaphoreType.DMA((NSA_KV_HEADS,))],
    )
    out = pl.pallas_call(
        _attn_sample_kernel,
        grid_spec=grid_spec,
        out_shape=jax.ShapeDtypeStruct((n_b, 1, NSA_QD), F32),
        compiler_params=_cparams(("arbitrary",)),
        name="nsa_attn_sample",
    )(page_ids, flags[:, :, 0, :], q_s.reshape(n_b, 1, NSA_QD), gates_s.reshape(n_b, 1, NSA_GATE_LANES), o_cmp,
      rows_new.reshape(n_b, 1, 4 * NSA_KVD), win_new.reshape(n_b, 1, 2 * NSA_KVD), wc, cache2)
    return out.reshape(n_b, NSA_QD)


DIL_QD = DIL_GROUPS * DIL_HEADS * HEAD_DIM
DIL_GD = DIL_HEADS * HEAD_DIM
DIL_TQ = 128
DIL_KC = 512


def _dil_proj_kernel(x_ref, g_ref, w_ref, qn_ref, kn_ref, q_ref, kv_ref, *, hp):
    y = _mm(_rms(x_ref[...], g_ref[...]), w_ref[...], hp)
    for p in range(DIL_QD // LANES):
        blk = y[:, p * LANES:(p + 1) * LANES]
        q_ref[:, p * LANES:(p + 1) * LANES] = (_pair_head_norm(blk, qn_ref[...]) * (HEAD_DIM ** -0.5)).astype(q_ref.dtype)
    for g in range(DIL_GROUPS):
        for p in range(DIL_GD // LANES):
            kblk = y[:, DIL_QD + g * DIL_GD + p * LANES:DIL_QD + g * DIL_GD + (p + 1) * LANES]
            vblk = y[:, 2 * DIL_QD + g * DIL_GD + p * LANES:2 * DIL_QD + g * DIL_GD + (p + 1) * LANES]
            kv_ref[:, g * 2 * DIL_GD + p * LANES:g * 2 * DIL_GD + (p + 1) * LANES] = _pair_head_norm(kblk, kn_ref[...])
            kv_ref[:, (g * 2 + 1) * DIL_GD + p * LANES:(g * 2 + 1) * DIL_GD + (p + 1) * LANES] = vblk


def dil_proj(x, g, w_in, q_norm, k_norm, hp=False):
    m, d = x.shape
    tm = _tile(m)
    qn = jnp.tile(q_norm, 2).reshape(1, LANES)
    kn = jnp.tile(k_norm, 2).reshape(1, LANES)
    return pl.pallas_call(
        functools.partial(_dil_proj_kernel, hp=hp),
        grid=(m // tm,),
        in_specs=[pl.BlockSpec((tm, d), lambda i: (i, 0)),
                  pl.BlockSpec((1, d), lambda i: (0, 0)),
                  pl.BlockSpec((d, 3 * DIL_QD), lambda i: (0, 0)),
                  pl.BlockSpec((1, LANES), lambda i: (0, 0)),
                  pl.BlockSpec((1, LANES), lambda i: (0, 0))],
        out_specs=[pl.BlockSpec((tm, DIL_QD), lambda i: (i, 0)),
                   pl.BlockSpec((tm, 2 * DIL_QD), lambda i: (i, 0))],
        out_shape=[jax.ShapeDtypeStruct((m, DIL_QD), _wdtype(hp)),
                   jax.ShapeDtypeStruct((m, 2 * DIL_QD), F32)],
        compiler_params=_cparams(("parallel",)),
        name="dil_proj",
    )(x, g.reshape(1, d), w_in.astype(_wdtype(hp)), qn, kn)


def _dil_prompt_kernel(q0_ref, q1_ref, q2_ref, k0_ref, v0_ref, k1_ref, v1_ref, k2_ref, v2_ref, o_ref,
                       k0_bf, v0_bf, k1_bf, v1_bf, k2_bf, v2_bf):
    i = pl.program_id(2)
    tq = DIL_TQ

    @pl.when(i == 0)
    def _():
        for src, dst in ((k0_ref, k0_bf), (v0_ref, v0_bf), (k1_ref, k1_bf), (v1_ref, v1_bf), (k2_ref, k2_bf), (v2_ref, v2_bf)):
            dst[...] = src[...].astype(BF16)

    t0 = i * tq
    t = t0 + lax.broadcasted_iota(jnp.int32, (tq, 1), 0)
    lo128 = _lane_is_lo((tq, LANES))
    q_refs = (q0_ref, q1_ref, q2_ref)
    k_bfs = (k0_bf, k1_bf, k2_bf)
    v_bfs = (v0_bf, v1_bf, v2_bf)

    def dil_mask(start, n, w, r):
        d = t - (start + lax.broadcasted_iota(jnp.int32, (tq, n), 1))
        return (d >= 0) & (d <= w) & ((d & (r - 1)) == 0)

    halves = []
    for hh in range(2):
        sel = lo128 if hh == 0 else ~lo128
        qh = [jnp.where(sel, q_refs[g][...].astype(F32), 0.0).astype(BF16) for g in range(DIL_GROUPS)]
        carry = (jnp.full((tq, 1), NEG, F32), jnp.zeros((tq, 1), F32), jnp.zeros((tq, LANES), F32))
        for g in range(DIL_GROUPS - 1):
            w, r = DIL_PAIRS[g]
            n = w + tq
            start = pl.multiple_of(jnp.maximum(t0 - w, 0), tq)
            s = jnp.where(dil_mask(start, n, w, r), _dot_nt(qh[g], k_bfs[g][pl.ds(start, n), :]), NEG)
            carry = _online_update(carry, s, v_bfs[g][pl.ds(start, n), :])
        w, r = DIL_PAIRS[DIL_GROUPS - 1]
        g = DIL_GROUPS - 1

        def chunk(c, carry):
            c0 = pl.multiple_of(c * DIL_KC, DIL_KC)
            s = jnp.where(dil_mask(c0, DIL_KC, w, r), _dot_nt(qh[g], k_bfs[g][pl.ds(c0, DIL_KC), :]), NEG)
            return _online_update(carry, s, v_bfs[g][pl.ds(c0, DIL_KC), :])

        _, l_f, acc_f = lax.fori_loop(0, (t0 + tq + DIL_KC - 1) // DIL_KC, chunk, carry)
        halves.append(acc_f / l_f)
    o_ref[...] = jnp.where(lo128, halves[0], halves[1]).astype(o_ref.dtype)


def dil_prompt_attention(q, kv, n_b, seq):
    tq = DIL_TQ
    nq = seq // tq
    n_pairs = DIL_GD // LANES
    assert DIL_PAIRS[-1][0] >= seq and seq % DIL_KC == 0
    qspec = lambda g: pl.BlockSpec((tq, LANES), lambda b, p, i: (b * nq + i, g * n_pairs + p))
    kvspec = lambda j: pl.BlockSpec((seq, LANES), lambda b, p, i: (b, j * n_pairs + p))
    return pl.pallas_call(
        _dil_prompt_kernel,
        grid=(n_b, n_pairs, nq),
        in_specs=[qspec(0), qspec(1), qspec(2)] + [kvspec(j) for j in range(2 * DIL_GROUPS)],
        out_specs=pl.BlockSpec((tq, LANES), lambda b, p, i: (b * nq + i, p)),
        out_shape=jax.ShapeDtypeStruct((n_b * seq, DIL_GD), BF16),
        scratch_shapes=[pltpu.VMEM((seq, LANES), BF16)] * (2 * DIL_GROUPS),
        compiler_params=_cparams(("arbitrary", "arbitrary", "arbitrary")),
        name="dil_prompt_attention",
    )(q, q, q, *([kv] * (2 * DIL_GROUPS)))


def _dil_sample_kernel(q_ref, new_ref, c0_ref, c1_ref, c2_ref, o_ref):
    q_row = q_ref[0]
    new_row = new_ref[0]
    caches = (c0_ref, c1_ref, c2_ref)
    lo = _lane_is_lo((ROWS8, LANES))
    row = lax.broadcasted_iota(jnp.int32, (ROWS8, LANES), 0)
    for p in range(DIL_GD // LANES):
        m_run = jnp.full((ROWS8, 1), NEG, F32)
        l_run = jnp.zeros((ROWS8, 1), F32)
        acc = jnp.zeros((ROWS8, LANES), F32)
        for g in range(DIL_GROUPS):
            qp = jnp.broadcast_to(q_row[:, g * DIL_GD + p * LANES:g * DIL_GD + (p + 1) * LANES], (ROWS8, LANES))
            q8 = jnp.where((row == 0) & lo, qp, jnp.where((row == 1) & ~lo, qp, 0.0))
            num, den, m_g = _attend_extra(
                q8, caches[g][0, :, p * LANES:(p + 1) * LANES], caches[g][0, :, DIL_GD + p * LANES:DIL_GD + (p + 1) * LANES],
                None,
                new_row[:, g * 2 * DIL_GD + p * LANES:g * 2 * DIL_GD + (p + 1) * LANES],
                new_row[:, (g * 2 + 1) * DIL_GD + p * LANES:(g * 2 + 1) * DIL_GD + (p + 1) * LANES])
            m_new = jnp.maximum(m_run, m_g)
            a_old = jnp.exp(m_run - m_new)
            a_g = jnp.exp(m_g - m_new)
            l_run = a_old * l_run + a_g * den
            acc = a_old * acc + a_g * num
            m_run = m_new
        o = acc / l_run
        o_ref[0, :, p * LANES:(p + 1) * LANES] = jnp.where(lo[0:1], o[0:1], o[1:2])


def dil_sample_attention(q_s, kv_new, caches):
    n_b = q_s.shape[0]
    views = []
    for c, (w, r) in zip(caches, DIL_PAIRS):
        assert c.shape[1] == w
        views.append(c.reshape(n_b, w // r, r * 2 * DIL_GD))
    cspec = lambda w, r: pl.BlockSpec((1, w // r, 2 * DIL_GD), lambda b: (b, 0, 0))
    out = pl.pallas_call(
        _dil_sample_kernel,
        grid=(n_b,),
        in_specs=[pl.BlockSpec((1, 1, DIL_QD), lambda b: (b, 0, 0)),
                  pl.BlockSpec((1, 1, 2 * DIL_QD), lambda b: (b, 0, 0))] + [cspec(w, r) for w, r in DIL_PAIRS],
        out_specs=pl.BlockSpec((1, 1, DIL_GD), lambda b: (b, 0, 0)),
        out_shape=jax.ShapeDtypeStruct((n_b, 1, DIL_GD), F32),
        compiler_params=_cparams(("parallel",)),
        name="dil_sample_attention",
    )(q_s.reshape(n_b, 1, DIL_QD), kv_new.reshape(n_b, 1, 2 * DIL_QD), *views)
    return out.reshape(n_b, DIL_GD)


def kernel(x_prompt, x_sample, cache_nsa_kv, cache_nsa_win, state_lru_conv, state_lru_h, cache_dil_win0, cache_dil_win1, cache_dil_win2, page_table, norm_mix, norm_ffn, nsa_w_in, nsa_q_norm, nsa_k_norm, nsa_cmp_w1, nsa_cmp_b1, nsa_cmp_w2, nsa_cmp_b2, nsa_w_out, lru_w_in, lru_conv_w, lru_conv_b, lru_gate_a_w, lru_gate_a_b, lru_gate_x_w, lru_gate_x_b, lru_lambda, lru_w_out, dil_w_in, dil_q_norm, dil_k_norm, dil_w_out, moe_router_group_w, moe_router_group_b, moe_router_expert_w, moe_router_expert_b, moe_w_in, moe_w_out):
    depth = norm_mix.shape[0]
    n_pr, seq, d = x_prompt.shape
    n_dec, dec_seq, _ = x_sample.shape
    assert dec_seq == 1
    past_len = page_table.shape[1] * PAGE_SIZE
    dil_caches = (cache_dil_win0, cache_dil_win1, cache_dil_win2)
    n_p = n_pr * seq
    xp = x_prompt.reshape(n_p, d)
    xs = x_sample.reshape(n_dec, d)
    n_pool = cache_nsa_kv.shape[1]
    cache2 = cache_nsa_kv.reshape(cache_nsa_kv.shape[0] * n_pool, PAGE_SIZE, 4 * NSA_KVD)

    kv_p, kv_s, nw_p, nw_s = [], [], [], []
    cv_p, cv_s, hh_p, hh_s = [], [], [], []
    dw_p, dw_s = ([], [], []), ([], [], [])
    for layer in range(depth):
        j = layer // N_MIXERS
        if layer % N_MIXERS == 0:
            phi = (nsa_k_norm[j][0], nsa_cmp_w1[j], nsa_cmp_b1[j], nsa_cmp_w2[j], nsa_cmp_b2[j])
            q, gates, rows, win = nsa_proj(xp, norm_mix[layer], nsa_w_in[j], nsa_q_norm[j], nsa_k_norm[j])
            cmp = cmp_prompt(rows, n_pr, seq, *phi)
            op = nsa_prompt_attention(q, gates, cmp, rows, win, n_pr, seq)
            xp = matmul_res(op, nsa_w_out[j], xp)
            q_s, g_s, rows_s, win_s = nsa_proj(xs, norm_mix[layer], nsa_w_in[j], nsa_q_norm[j], nsa_k_norm[j], hp=True)
            page_ids = page_table + j * n_pool
            cmp_s = cmp_sample(cache2, page_ids, rows_s, *phi)
            o_cmp, flags = sel_sample(q_s, cmp_s, past_len)
            os_ = attn_sample(q_s, g_s, o_cmp, flags, rows_s, win_s, cache_nsa_win[j], cache2, page_ids)
            xs = matmul_res(os_, nsa_w_out[j], xs, hp=True)
            kv_p.append(rows.reshape(n_pr, seq, 4, NSA_KV_HEADS, HEAD_DIM))
            kv_s.append(rows_s.reshape(n_dec, dec_seq, 4, NSA_KV_HEADS, HEAD_DIM))
            nw_p.append(win.reshape(n_pr, seq, 2, NSA_KV_HEADS, HEAD_DIM)[:, seq - min(NSA_WINDOW, seq):])
            nw_s.append(jnp.concatenate([cache_nsa_win[j][:, dec_seq:],
                                         win_s.reshape(n_dec, dec_seq, 2, NSA_KV_HEADS, HEAD_DIM)], axis=1))
        elif layer % N_MIXERS == 1:
            lp = (lru_conv_w[j], lru_conv_b[j], lru_gate_a_w[j], lru_gate_a_b[j],
                  lru_gate_x_w[j], lru_gate_x_b[j], lru_lambda[j])
            xy_p = norm_proj(xp, norm_mix[layer], lru_w_in[j]).reshape(n_pr, seq, 2 * D_RNN)
            op, h_p = lru_prompt(xy_p, *lp)
            xp = matmul_res(op.reshape(n_p, D_RNN), lru_w_out[j], xp)
            xy_s = norm_proj(xs, norm_mix[layer], lru_w_in[j], hp=True)
            os_, h_s = lru_step(xy_s, jnp.swapaxes(state_lru_conv[j], 0, 1), state_lru_h[j], *lp)
            xs = matmul_res(os_, lru_w_out[j], xs, hp=True)
            cv_p.append(xy_p[:, seq - (CONV_W - 1):, :D_RNN])
            cv_s.append(jnp.concatenate([state_lru_conv[j], xy_s[:, None, :D_RNN]], axis=1)[:, -(CONV_W - 1):])
            hh_p.append(h_p[:, 0])
            hh_s.append(h_s)
        else:
            q, kv = dil_proj(xp, norm_mix[layer], dil_w_in[j], dil_q_norm[j], dil_k_norm[j])
            op = dil_prompt_attention(q, kv, n_pr, seq)
            xp = matmul_res(op, dil_w_out[j], xp)
            q_s, kv_s_new = dil_proj(xs, norm_mix[layer], dil_w_in[j], dil_q_norm[j], dil_k_norm[j], hp=True)
            os_ = dil_sample_attention(q_s, kv_s_new, [c[j] for c in dil_caches])
            xs = matmul_res(os_, dil_w_out[j], xs, hp=True)
            kvp = kv.reshape(n_pr, seq, DIL_GROUPS, 2, DIL_HEADS, HEAD_DIM)
            kvs = kv_s_new.reshape(n_dec, dec_seq, DIL_GROUPS, 2, DIL_HEADS, HEAD_DIM)
            for g, (w, _) in enumerate(DIL_PAIRS):
                dw_p[g].append(kvp[:, seq - min(w, seq):, g])
                dw_s[g].append(jnp.concatenate([dil_caches[g][j][:, dec_seq:], kvs[:, :, g]], axis=1))
        moe = (norm_ffn[layer], moe_router_group_w[layer], moe_router_group_b[layer],
               moe_router_expert_w[layer], moe_router_expert_b[layer], moe_w_in[layer], moe_w_out[layer])
        xp = hier_moe_res(xp, *moe)
        xs = hier_moe_res(xs, *moe, hp=True)
    return (xp.reshape(n_pr, seq, d), xs.reshape(n_dec, dec_seq, d),
            jnp.stack(kv_p), jnp.stack(kv_s), jnp.stack(nw_p), jnp.stack(nw_s),
            jnp.stack(cv_p), jnp.stack(cv_s), jnp.stack(hh_p), jnp.stack(hh_s),
            jnp.stack(dw_p[0]), jnp.stack(dw_s[0]), jnp.stack(dw_p[1]), jnp.stack(dw_s[1]),
            jnp.stack(dw_p[2]), jnp.stack(dw_s[2]))
```

```python
import functools

import numpy as np
import jax
import jax.numpy as jnp
from jax import lax
from jax.experimental import pallas as pl
from jax.experimental.pallas import tpu as pltpu

F32 = jnp.float32
BF16 = jnp.bfloat16

D_MODEL = 1024
N_MIXERS = 3
EPS = 1e-6
NEG = -1e30
BIG = 1e9
TINY = 1e-30
HEAD_DIM = 64

NSA_HEADS = 16
NSA_KV_HEADS = 4
NSA_GROUP = NSA_HEADS // NSA_KV_HEADS
CMP_BLOCK = 32
CMP_STRIDE = 16
SEL_BLOCK = 64
SEL_TOP = 16
NSA_WINDOW = 512
PHI_HIDDEN = 128

D_RNN = D_MODEL
CONV_W = 4
LRU_BLOCKS = 4
LRU_BLOCK_W = D_RNN // LRU_BLOCKS
LRU_C = 8.0

DIL_PAIRS = ((128, 1), (512, 4), (2048, 16))
DIL_GROUPS = 3
DIL_HEADS = 8

N_EXPERT_GROUPS = 4
EXPERTS_PER_GROUP = 4
N_EXPERTS = N_EXPERT_GROUPS * EXPERTS_PER_GROUP
D_EXPERT = 512

PAGE_SIZE = 128

LANES = 128
SUBLANES = 8
VMEM_LIMIT = 56 * 1024 * 1024

TOKEN_TILE = 256
MOE_TILE = 1024
LRU_CHUNK = 256


def _cparams(sem):
    return pltpu.CompilerParams(dimension_semantics=sem, vmem_limit_bytes=VMEM_LIMIT)


def _rms(x, g):
    ms = jnp.mean(x * x, axis=-1, keepdims=True)
    return x * lax.rsqrt(ms + EPS) * g


def _split2(x):
    hi = x.astype(BF16)
    return hi, (x - hi.astype(F32)).astype(BF16)


def _split2_masked(x):
    hi = lax.bitcast_convert_type(lax.bitcast_convert_type(x, jnp.uint32) & jnp.uint32(0xFFFF0000), F32)
    return hi.astype(BF16), (x - hi).astype(BF16)


def _split3(x):
    h1 = x.astype(BF16)
    r1 = x - h1.astype(F32)
    h2 = r1.astype(BF16)
    h3 = (r1 - h2.astype(F32)).astype(BF16)
    return h1, h2, h3


def _dot_nn(a, b):
    return jnp.dot(a, b, preferred_element_type=F32)


def _dot_nt(a, b):
    return lax.dot_general(a, b, (((1,), (1,)), ((), ())), preferred_element_type=F32)


def _dot3_parts(a, b, dot):
    return dot(a[0], b[0]) + (dot(a[0], b[1]) + dot(a[1], b[0]))


def _mm(a, w, hp):
    if hp:
        return _dot3_parts(_split2(a.astype(F32)), _split2(w.astype(F32)), _dot_nn)
    return _dot_nn(a.astype(BF16), w.astype(BF16))


def _mm_nt(a, b, hp):
    if hp:
        return _dot3_parts(_split2(a.astype(F32)), _split2(b.astype(F32)), _dot_nt)
    return _dot_nt(a.astype(BF16), b.astype(BF16))


def _wdtype(hp):
    return F32 if hp else BF16


def _tile(m):
    return min(TOKEN_TILE, m)


def _norm_proj_kernel(x_ref, g_ref, w_ref, o_ref, *, hp):
    o_ref[...] = _mm(_rms(x_ref[...], g_ref[...]), w_ref[...], hp)


def norm_proj(x, g, w, hp=False):
    m, d = x.shape
    n = w.shape[1]
    tm = _tile(m)
    return pl.pallas_call(
        functools.partial(_norm_proj_kernel, hp=hp),
        grid=(m // tm,),
        in_specs=[pl.BlockSpec((tm, d), lambda i: (i, 0)),
                  pl.BlockSpec((1, d), lambda i: (0, 0)),
                  pl.BlockSpec((d, n), lambda i: (0, 0))],
        out_specs=pl.BlockSpec((tm, n), lambda i: (i, 0)),
        out_shape=jax.ShapeDtypeStruct((m, n), F32),
        compiler_params=_cparams(("parallel",)),
        name="norm_proj",
    )(x, g.reshape(1, d), w.astype(_wdtype(hp)))


def _matmul_res_kernel(a_ref, w_ref, r_ref, o_ref, *, hp):
    o_ref[...] = r_ref[...] + _mm(a_ref[...], w_ref[...], hp)


def matmul_res(a, w, res, hp=False):
    m, k = a.shape
    n = w.shape[1]
    tm = _tile(m)
    return pl.pallas_call(
        functools.partial(_matmul_res_kernel, hp=hp),
        grid=(m // tm,),
        in_specs=[pl.BlockSpec((tm, k), lambda i: (i, 0)),
                  pl.BlockSpec((k, n), lambda i: (0, 0)),
                  pl.BlockSpec((tm, n), lambda i: (i, 0))],
        out_specs=pl.BlockSpec((tm, n), lambda i: (i, 0)),
        out_shape=jax.ShapeDtypeStruct((m, n), F32),
        compiler_params=_cparams(("parallel",)),
        name="matmul_res",
    )(a, w.astype(_wdtype(hp)), res)


ROUTER_LANES = LANES


def _router_kernel(x_ref, g_ref, w_ref, b_ref, xn_ref, comb_ref):
    xn = _rms(x_ref[...], g_ref[...])
    xn_ref[...] = xn.astype(xn_ref.dtype)
    z = _mm(xn, w_ref[...], True) + b_ref[...]
    lane = lax.broadcasted_iota(jnp.int32, z.shape, 1)
    is_g = lane < N_EXPERT_GROUPS
    ninf = jnp.float32(-jnp.inf)
    far = jnp.int32(4 * LANES)
    zg = jnp.where(is_g, z, ninf)
    mg = jnp.max(zg, axis=-1, keepdims=True)
    g_top = jnp.min(jnp.where(is_g & (z == mg), lane, far), axis=-1, keepdims=True)
    den = jnp.sum(jnp.where(is_g, jnp.exp(zg - mg), 0.0), axis=-1, keepdims=True)
    g_gate = 1.0 / den
    lo = N_EXPERT_GROUPS + EXPERTS_PER_GROUP * g_top
    sel = (lane >= lo) & (lane < lo + EXPERTS_PER_GROUP)
    z1 = jnp.where(sel, z, ninf)
    v1 = jnp.max(z1, axis=-1, keepdims=True)
    i1 = jnp.min(jnp.where(sel & (z == v1), lane, far), axis=-1, keepdims=True)
    sel2 = sel & (lane != i1)
    z2 = jnp.where(sel2, z, ninf)
    v2 = jnp.max(z2, axis=-1, keepdims=True)
    i2 = jnp.min(jnp.where(sel2 & (z == v2), lane, far), axis=-1, keepdims=True)
    e2 = jnp.exp(v2 - v1)
    s = 1.0 + e2
    comb_ref[...] = jnp.where(lane == i1, g_gate * (1.0 / s),
                              jnp.where(lane == i2, g_gate * (e2 / s), 0.0))


def moe_router(x, g, w, b, hp):
    m, d = x.shape
    tm = _tile(m)
    return pl.pallas_call(
        _router_kernel,
        grid=(m // tm,),
        in_specs=[pl.BlockSpec((tm, d), lambda i: (i, 0)),
                  pl.BlockSpec((1, d), lambda i: (0, 0)),
                  pl.BlockSpec((d, ROUTER_LANES), lambda i: (0, 0)),
                  pl.BlockSpec((1, ROUTER_LANES), lambda i: (0, 0))],
        out_specs=[pl.BlockSpec((tm, d), lambda i: (i, 0)),
                   pl.BlockSpec((tm, ROUTER_LANES), lambda i: (i, 0))],
        out_shape=[jax.ShapeDtypeStruct((m, d), _wdtype(hp)),
                   jax.ShapeDtypeStruct((m, ROUTER_LANES), F32)],
        compiler_params=_cparams(("parallel",)),
        name="moe_router",
    )(x, g.reshape(1, d), w, b)


def _moe_dense_kernel(xn_ref, comb_ref, res_ref, win_ref, wout_ref, o_ref, acc_ref, *, hp):
    e = pl.program_id(1)

    @pl.when(e == 0)
    def _():
        acc_ref[...] = jnp.zeros_like(acc_ref)

    gu = _mm(xn_ref[...], win_ref[0], hp)
    gate, up = gu[:, :D_EXPERT], gu[:, D_EXPERT:]
    act = (gate * jax.nn.sigmoid(gate)) * up
    y = _mm(act, wout_ref[0], hp)
    comb = comb_ref[...]
    lane = lax.broadcasted_iota(jnp.int32, comb.shape, 1)
    ce = jnp.sum(jnp.where(lane == N_EXPERT_GROUPS + e, comb, 0.0), axis=-1, keepdims=True)
    acc_ref[...] += ce * y

    @pl.when(e == N_EXPERTS - 1)
    def _():
        o_ref[...] = res_ref[...] + acc_ref[...]


def moe_dense(xn, comb, res, w_in, w_out, hp):
    m, d = xn.shape
    tm = min(MOE_TILE, m)
    return pl.pallas_call(
        functools.partial(_moe_dense_kernel, hp=hp),
        grid=(m // tm, N_EXPERTS),
        in_specs=[pl.BlockSpec((tm, d), lambda i, e: (i, 0)),
                  pl.BlockSpec((tm, ROUTER_LANES), lambda i, e: (i, 0)),
                  pl.BlockSpec((tm, d), lambda i, e: (i, 0)),
                  pl.BlockSpec((1, d, 2 * D_EXPERT), lambda i, e: (e, 0, 0)),
                  pl.BlockSpec((1, D_EXPERT, d), lambda i, e: (e, 0, 0))],
        out_specs=pl.BlockSpec((tm, d), lambda i, e: (i, 0)),
        out_shape=jax.ShapeDtypeStruct((m, d), F32),
        scratch_shapes=[pltpu.VMEM((tm, d), F32)],
        compiler_params=_cparams(("parallel", "arbitrary")),
        name="moe_dense",
    )(xn, comb, res, w_in, w_out)


def hier_moe_res(x, g, rg_w, rg_b, re_w, re_b, w_in, w_out, hp=False):
    d = x.shape[1]
    n_log = N_EXPERT_GROUPS + N_EXPERTS
    w = jnp.zeros((d, ROUTER_LANES), F32).at[:, :N_EXPERT_GROUPS].set(rg_w).at[:, N_EXPERT_GROUPS:n_log].set(re_w)
    b = jnp.zeros((1, ROUTER_LANES), F32).at[0, :N_EXPERT_GROUPS].set(rg_b).at[0, N_EXPERT_GROUPS:n_log].set(re_b)
    xn, comb = moe_router(x, g, w, b, hp)
    return moe_dense(xn, comb, x, w_in.astype(_wdtype(hp)), w_out.astype(_wdtype(hp)), hp)


def _expm1(z):
    u = jnp.exp(z)
    d = u - 1.0
    comp = d * z / jnp.log(u)
    return jnp.where(d == 0.0, z, jnp.where(z < -1.0, d, comp))


def _gelu(x):
    return jax.nn.gelu(x)


def _lru_gates(xc, ga_w_ref, ga_b, gx_w_ref, gx_b, lam, hp):
    rs, is_ = [], []
    for n in range(LRU_BLOCKS):
        xg = xc[:, n * LRU_BLOCK_W:(n + 1) * LRU_BLOCK_W]
        rs.append(_mm(xg, ga_w_ref[n], hp))
        is_.append(_mm(xg, gx_w_ref[n], hp))
    r = jax.nn.sigmoid(jnp.concatenate(rs, axis=-1) + ga_b)
    i = jax.nn.sigmoid(jnp.concatenate(is_, axis=-1) + gx_b)
    softplus_neg_lam = jnp.maximum(-lam, 0.0) + jnp.log1p(jnp.exp(-jnp.abs(lam)))
    log_a = -LRU_C * r * softplus_neg_lam
    a = jnp.exp(log_a)
    b = jnp.sqrt(-_expm1(2.0 * log_a)) * (i * xc)
    return a, b


def _lru_prompt_kernel(xy_ref, cw_ref, cb_ref, ga_w_ref, ga_b_ref, gx_w_ref, gx_b_ref, lam_ref,
                       o_ref, hlast_ref, xext_ref, a_ref, b_ref, hs_ref, h_ref):
    c = pl.program_id(1)
    tc = LRU_CHUNK

    @pl.when(c == 0)
    def _():
        xext_ref[0:SUBLANES, :] = jnp.zeros((SUBLANES, D_RNN), F32)
        h_ref[...] = jnp.zeros_like(h_ref)

    xb = xy_ref[0, :, :D_RNN]
    xext_ref[SUBLANES:, :] = xb
    xc = cb_ref[...]
    for k in range(CONV_W):
        off = SUBLANES - (CONV_W - 1) + k
        xc = xc + xext_ref[off:off + tc, :] * cw_ref[k:k + 1, :]
    xext_ref[0:SUBLANES, :] = xb[tc - SUBLANES:, :]
    a, b = _lru_gates(xc, ga_w_ref, ga_b_ref[...], gx_w_ref, gx_b_ref[...], lam_ref[...], False)
    a_ref[...] = a
    b_ref[...] = b

    def step(t, h):
        h = a_ref[pl.ds(t, 1), :] * h + b_ref[pl.ds(t, 1), :]
        hs_ref[pl.ds(t, 1), :] = h
        return h

    h = lax.fori_loop(0, tc, step, h_ref[...], unroll=8)
    h_ref[...] = h
    hlast_ref[0] = h
    o_ref[0] = (hs_ref[...] * _gelu(xy_ref[0, :, D_RNN:])).astype(o_ref.dtype)


def lru_prompt(xy, cw, cb, ga_w, ga_b, gx_w, gx_b, lam):
    bsz, t, _ = xy.shape
    tc = LRU_CHUNK
    row = lambda v: v.reshape(1, D_RNN)
    full2 = lambda shp: pl.BlockSpec(shp, lambda b, c: (0, 0))
    full3 = lambda shp: pl.BlockSpec(shp, lambda b, c: (0, 0, 0))
    return pl.pallas_call(
        _lru_prompt_kernel,
        grid=(bsz, t // tc),
        in_specs=[pl.BlockSpec((1, tc, 2 * D_RNN), lambda b, c: (b, c, 0)),
                  full2((CONV_W, D_RNN)), full2((1, D_RNN)),
                  full3((LRU_BLOCKS, LRU_BLOCK_W, LRU_BLOCK_W)), full2((1, D_RNN)),
                  full3((LRU_BLOCKS, LRU_BLOCK_W, LRU_BLOCK_W)), full2((1, D_RNN)),
                  full2((1, D_RNN))],
        out_specs=[pl.BlockSpec((1, tc, D_RNN), lambda b, c: (b, c, 0)),
                   pl.BlockSpec((1, 1, D_RNN), lambda b, c: (b, 0, 0))],
        out_shape=[jax.ShapeDtypeStruct((bsz, t, D_RNN), BF16),
                   jax.ShapeDtypeStruct((bsz, 1, D_RNN), F32)],
        scratch_shapes=[pltpu.VMEM((tc + SUBLANES, D_RNN), F32),
                        pltpu.VMEM((tc, D_RNN), F32),
                        pltpu.VMEM((tc, D_RNN), F32),
                        pltpu.VMEM((tc, D_RNN), F32),
                        pltpu.VMEM((1, D_RNN), F32)],
        compiler_params=_cparams(("parallel", "arbitrary")),
        name="lru_prompt",
    )(xy, cw, row(cb), ga_w.astype(BF16), row(ga_b), gx_w.astype(BF16), row(gx_b), row(lam))


def _lru_step_kernel(xy_ref, cbuf_ref, h0_ref, cw_ref, cb_ref, ga_w_ref, ga_b_ref, gx_w_ref, gx_b_ref, lam_ref,
                     o_ref, h_ref):
    xb = xy_ref[:, :D_RNN]
    xc = cb_ref[...]
    for k in range(CONV_W - 1):
        xc = xc + cbuf_ref[k] * cw_ref[k:k + 1, :]
    xc = xc + xb * cw_ref[CONV_W - 1:CONV_W, :]
    a, b = _lru_gates(xc, ga_w_ref, ga_b_ref[...], gx_w_ref, gx_b_ref[...], lam_ref[...], True)
    h = a * h0_ref[...] + b
    h_ref[...] = h
    o_ref[...] = h * _gelu(xy_ref[:, D_RNN:])


def lru_step(xy, cbuf, h0, cw, cb, ga_w, ga_b, gx_w, gx_b, lam):
    bsz = xy.shape[0]
    row = lambda v: v.reshape(1, D_RNN)
    return pl.pallas_call(
        _lru_step_kernel,
        out_shape=[jax.ShapeDtypeStruct((bsz, D_RNN), F32),
                   jax.ShapeDtypeStruct((bsz, D_RNN), F32)],
        compiler_params=pltpu.CompilerParams(vmem_limit_bytes=VMEM_LIMIT),
        name="lru_step",
    )(xy, cbuf, h0, cw, row(cb), ga_w, row(ga_b), gx_w, row(gx_b), row(lam))


HALF = HEAD_DIM
ROWS8 = SUBLANES


def _lane_is_lo(shape):
    return lax.broadcasted_iota(jnp.int32, shape, len(shape) - 1) < HALF


def _pair_head_norm(y, gamma):
    sq = y * y
    lo = _lane_is_lo(y.shape)
    ss_lo = jnp.sum(jnp.where(lo, sq, 0.0), axis=-1, keepdims=True)
    ss_hi = jnp.sum(jnp.where(lo, 0.0, sq), axis=-1, keepdims=True)
    ms = jnp.where(lo, ss_lo, ss_hi) * (1.0 / HEAD_DIM)
    return y * lax.rsqrt(ms + EPS) * gamma


def _roll_half(x):
    return pltpu.roll(x, HALF, axis=x.ndim - 1)


def _online_update(carry, s, vblk):
    m_old, l_old, acc_old = carry
    m_new = jnp.maximum(m_old, jnp.max(s, axis=-1, keepdims=True))
    alpha = jnp.exp(m_old - m_new)
    e = jnp.exp(s - m_new)
    l_new = alpha * l_old + jnp.sum(e, axis=-1, keepdims=True)
    acc_new = alpha * acc_old + _dot_nn(e.astype(BF16), vblk)
    return m_new, l_new, acc_new


def _attend_extra(q8, keys, vals, kmask, k_new, v_new):
    q_parts = _split2(q8)
    s = _dot3_parts(q_parts, _split2(keys), _dot_nt)
    if kmask is not None:
        s = jnp.where(kmask, s, NEG)
    s_new = jnp.sum(q8 * k_new, axis=-1, keepdims=True)
    m = jnp.maximum(jnp.max(s, axis=-1, keepdims=True), s_new)
    e = jnp.exp(s - m)
    e_new = jnp.exp(s_new - m)
    den = jnp.sum(e, axis=-1, keepdims=True) + e_new
    num = _dot3_parts(_split2(e), _split2(vals), _dot_nn) + e_new * v_new
    return num, den, m


NSA_QD = NSA_HEADS * HEAD_DIM
NSA_KVD = NSA_KV_HEADS * HEAD_DIM
NSA_GATE_LANES = LANES
NSA_PROJ_COLS = NSA_QD + 6 * NSA_KVD + NSA_GATE_LANES


def _nsa_proj_kernel(x_ref, g_ref, w_ref, qn_ref, kn_ref, q_ref, gate_ref, rows_ref, win_ref, *, hp):
    y = _mm(_rms(x_ref[...], g_ref[...]), w_ref[...], hp)
    qn = qn_ref[...]
    for p in range(NSA_QD // LANES):
        blk = y[:, p * LANES:(p + 1) * LANES]
        q_ref[:, p * LANES:(p + 1) * LANES] = (_pair_head_norm(blk, qn) * (HEAD_DIM ** -0.5)).astype(q_ref.dtype)
    kv0 = NSA_QD
    for c in range(6):
        for p in range(NSA_KVD // LANES):
            lo = kv0 + c * NSA_KVD + p * LANES
            blk = y[:, lo:lo + LANES]
            if c == 2:
                blk = _pair_head_norm(blk, kn_ref[0:1, :])
            elif c == 4:
                blk = _pair_head_norm(blk, kn_ref[1:2, :])
            if c < 4:
                rows_ref[:, c * NSA_KVD + p * LANES:c * NSA_KVD + (p + 1) * LANES] = blk
            else:
                win_ref[:, (c - 4) * NSA_KVD + p * LANES:(c - 4) * NSA_KVD + (p + 1) * LANES] = blk
    gate_ref[...] = jax.nn.sigmoid(y[:, kv0 + 6 * NSA_KVD:])


def nsa_proj(x, g, w_in, q_norm, k_norm, hp=False):
    m, d = x.shape
    n_gate = 3 * NSA_HEADS
    w = jnp.concatenate([w_in[:, :NSA_QD], w_in[:, NSA_QD + n_gate:], w_in[:, NSA_QD:NSA_QD + n_gate],
                         jnp.zeros((d, NSA_GATE_LANES - n_gate), w_in.dtype)], axis=1).astype(_wdtype(hp))
    qn = jnp.tile(q_norm, 2).reshape(1, LANES)
    kn = jnp.stack([jnp.tile(k_norm[1], 2), jnp.tile(k_norm[2], 2)])
    tm = _tile(m)
    return pl.pallas_call(
        functools.partial(_nsa_proj_kernel, hp=hp),
        grid=(m // tm,),
        in_specs=[pl.BlockSpec((tm, d), lambda i: (i, 0)),
                  pl.BlockSpec((1, d), lambda i: (0, 0)),
                  pl.BlockSpec((d, NSA_PROJ_COLS), lambda i: (0, 0)),
                  pl.BlockSpec((1, LANES), lambda i: (0, 0)),
                  pl.BlockSpec((2, LANES), lambda i: (0, 0))],
        out_specs=[pl.BlockSpec((tm, NSA_QD), lambda i: (i, 0)),
                   pl.BlockSpec((tm, NSA_GATE_LANES), lambda i: (i, 0)),
                   pl.BlockSpec((tm, 4 * NSA_KVD), lambda i: (i, 0)),
                   pl.BlockSpec((tm, 2 * NSA_KVD), lambda i: (i, 0))],
        out_shape=[jax.ShapeDtypeStruct((m, NSA_QD), _wdtype(hp)),
                   jax.ShapeDtypeStruct((m, NSA_GATE_LANES), F32),
                   jax.ShapeDtypeStruct((m, 4 * NSA_KVD), F32),
                   jax.ShapeDtypeStruct((m, 2 * NSA_KVD), F32)],
        compiler_params=_cparams(("parallel",)),
        name="nsa_proj",
    )(x, g.reshape(1, d), w, qn, kn)


def _cmp_weights(w1, b1, w2, b2):
    eye = jnp.eye(NSA_KV_HEADS, dtype=w1.dtype)
    lead_tail = jnp.concatenate([w1[:, :CMP_STRIDE], w1[:, CMP_STRIDE:]], axis=-1)
    w1bd = jnp.einsum('csdh,kj->cskdjh', lead_tail, eye).reshape(2, CMP_STRIDE, NSA_KVD, NSA_KV_HEADS * 2 * PHI_HIDDEN)
    w2bd = jnp.einsum('chd,kj->ckhjd', w2, eye).reshape(2, NSA_KV_HEADS * PHI_HIDDEN, NSA_KVD)
    b1t = jnp.tile(b1, (1, NSA_KV_HEADS)).reshape(2, 1, NSA_KV_HEADS * PHI_HIDDEN)
    b2t = jnp.tile(b2, (1, NSA_KV_HEADS)).reshape(2, 1, NSA_KVD)
    return w1bd, b1t, w2bd, b2t


def _compress_halves(first_layer, b1, second_layer, b2, knorm, is_key, extra_tail=None):
    acc = first_layer()
    n_half = acc.shape[0]
    hids = []
    for k in range(NSA_KV_HEADS):
        lead = acc[:, k * 2 * PHI_HIDDEN:k * 2 * PHI_HIDDEN + PHI_HIDDEN]
        tail = acc[:, k * 2 * PHI_HIDDEN + PHI_HIDDEN:(k + 1) * 2 * PHI_HIDDEN]
        nxt = pltpu.roll(tail, n_half - 1, axis=0)
        if extra_tail is not None:
            row = lax.broadcasted_iota(jnp.int32, nxt.shape, 0)
            nxt = jnp.where(row == n_half - 1, extra_tail[:, k * PHI_HIDDEN:(k + 1) * PHI_HIDDEN], nxt)
        hids.append(lead + nxt)
    hid = _gelu(jnp.concatenate(hids, axis=-1) + b1)
    out = second_layer(hid) + b2
    normed = jnp.concatenate([_pair_head_norm(out[:, p * LANES:(p + 1) * LANES], knorm)
                              for p in range(NSA_KVD // LANES)], axis=-1)
    return jnp.where(is_key, normed, out)


def _cmp_prompt_kernel(x_ref, w1_ref, b1_ref, w2_ref, b2_ref, kn_ref, o_ref):
    c = pl.program_id(1)

    def first_layer():
        acc = None
        for s in range(CMP_STRIDE):
            part = _dot_nn(x_ref[:, s, :].astype(BF16), w1_ref[0, s])
            acc = part if acc is None else acc + part
        return acc

    out = _compress_halves(first_layer, b1_ref[0], lambda hid: _dot_nn(hid.astype(BF16), w2_ref[0]), b2_ref[0],
                           kn_ref[...], c == 0)
    o_ref[0, 0] = out.astype(o_ref.dtype)


def cmp_prompt(rows, n_b, seq, k_cmp_norm, w1, b1, w2, b2):
    n_half = seq // CMP_STRIDE
    x3 = rows.reshape(rows.shape[0] // CMP_STRIDE, CMP_STRIDE, 4 * NSA_KVD)
    w1bd, b1t, w2bd, b2t = _cmp_weights(w1, b1, w2, b2)
    kn = jnp.tile(k_cmp_norm, 2).reshape(1, LANES)
    return pl.pallas_call(
        _cmp_prompt_kernel,
        grid=(n_b, 2),
        in_specs=[pl.BlockSpec((n_half, CMP_STRIDE, NSA_KVD), lambda b, c: (b, 0, c)),
                  pl.BlockSpec((1, CMP_STRIDE, NSA_KVD, w1bd.shape[-1]), lambda b, c: (c, 0, 0, 0)),
                  pl.BlockSpec((1, 1, b1t.shape[-1]), lambda b, c: (c, 0, 0)),
                  pl.BlockSpec((1, w2bd.shape[1], NSA_KVD), lambda b, c: (c, 0, 0)),
                  pl.BlockSpec((1, 1, NSA_KVD), lambda b, c: (c, 0, 0)),
                  pl.BlockSpec((1, LANES), lambda b, c: (0, 0))],
        out_specs=pl.BlockSpec((1, 1, n_half, NSA_KVD), lambda b, c: (b, c, 0, 0)),
        out_shape=jax.ShapeDtypeStruct((n_b, 2, n_half, NSA_KVD), BF16),
        compiler_params=_cparams(("parallel", "arbitrary")),
        name="nsa_cmp_prompt",
    )(x3, w1bd.astype(BF16), b1t, w2bd.astype(BF16), b2t, kn)


NSA_TQ = 128
NSA_KC = 256
NSA_WCHUNK = NSA_WINDOW + NSA_TQ


def _sel_map(n_sel, n_cmp, rows, cols):
    ratio = SEL_BLOCK // CMP_STRIDE
    span = CMP_BLOCK // CMP_STRIDE
    idx = np.arange(n_sel)[:, None] * ratio - (span - 1) + np.arange(ratio + span - 1)[None, :]
    wgt = np.convolve(np.ones(ratio), np.ones(span))[None, :] * ((idx >= 0) & (idx < n_cmp))
    m = np.zeros((rows, cols), np.float32)
    for j in range(n_sel):
        for a in range(idx.shape[1]):
            if wgt[j, a] != 0:
                m[idx[j, a], j] += wgt[j, a]
    return m


def _topk_mask_cols(score_t, n_top):
    n = score_t.shape[0]
    row = lax.broadcasted_iota(jnp.int32, score_t.shape, 0)
    rank = jnp.zeros(score_t.shape, F32)
    for i in range(n):
        ri = score_t[i:i + 1, :]
        beats = jnp.where(ri > score_t, 1.0, jnp.where((ri == score_t) & (row > i), 1.0, 0.0))
        rank = rank + beats
    return jnp.where(rank < n_top, 1.0, 0.0)


def _nsa_prompt_kernel(q_ref, gate_ref, kc_ref, vc_ref, ks_ref, vs_ref, kw_ref, vw_ref, map_ref, exp_ref,
                       o_ref, ks_bf, vs_bf, kw_bf, vw_bf):
    kv = pl.program_id(1)
    i = pl.program_id(2)
    tq = NSA_TQ
    seq = ks_ref.shape[0]
    n_sel = seq // SEL_BLOCK

    @pl.when(i == 0)
    def _():
        ks_bf[...] = ks_ref[...].astype(BF16)
        vs_bf[...] = vs_ref[...].astype(BF16)
        kw_bf[...] = kw_ref[...].astype(BF16)
        vw_bf[...] = vw_ref[...].astype(BF16)

    hv = kv % 2
    t0 = i * tq
    t = t0 + lax.broadcasted_iota(jnp.int32, (tq, 1), 0)
    lo128 = _lane_is_lo((tq, LANES))

    qs = []
    for g in range(NSA_GROUP):
        pair = q_ref[:, (g // 2) * LANES:(g // 2 + 1) * LANES].astype(F32)
        own = jnp.where(lo128 if g % 2 == 0 else ~lo128, pair, 0.0)
        qs.append(jnp.where(hv == g % 2, own, _roll_half(own)).astype(BF16))

    gates = gate_ref[...]
    glane = lax.broadcasted_iota(jnp.int32, gates.shape, 1)

    def gate_col(g, c):
        idx = (kv * NSA_GROUP + g) * 3 + c
        return jnp.sum(jnp.where(glane == idx, gates, 0.0), axis=-1, keepdims=True)

    kc = kc_ref[0, 0]
    vc = vc_ref[0, 0]
    n_cmp_pad = kc.shape[0]
    cmp_end = lax.broadcasted_iota(jnp.int32, (tq, n_cmp_pad), 1) * CMP_STRIDE + (CMP_BLOCK - 1)
    cmask = cmp_end <= t
    o_c = []
    imp = None
    for g in range(NSA_GROUP):
        s = jnp.where(cmask, _dot_nt(qs[g], kc), NEG)
        e = jnp.where(cmask, jnp.exp(s - jnp.max(s, axis=-1, keepdims=True)), 0.0)
        p = e * (1.0 / jnp.maximum(jnp.sum(e, axis=-1, keepdims=True), TINY))
        o_c.append(_dot_nn(p.astype(BF16), vc))
        imp = p if imp is None else imp + p

    p_blk = None
    for term in _split3(imp):
        part = _dot_nn(term, map_ref[...])
        p_blk = part if p_blk is None else p_blk + part
    blk = lax.broadcasted_iota(jnp.int32, (tq, n_cmp_pad), 1)
    cur = t // SEL_BLOCK
    forced = (blk == 0) | (blk == cur) | (blk == cur - 1)
    score = jnp.where(blk * SEL_BLOCK > t, NEG, jnp.where(forced, BIG, p_blk))
    score_t = score.T[0:n_sel, :]
    sel_t = _topk_mask_cols(score_t, min(SEL_TOP, n_sel))
    sel_t = jnp.concatenate([sel_t, jnp.zeros((n_cmp_pad - n_sel, tq), F32)], axis=0)
    sel = sel_t.T.astype(BF16)

    kcs = NSA_KC
    n_chunks = (t0 + tq + kcs - 1) // kcs

    def chunk(c, carry):
        c0 = pl.multiple_of(c * kcs, kcs)
        kblk = ks_bf[pl.ds(c0, kcs), :]
        vblk = vs_bf[pl.ds(c0, kcs), :]
        maskf = _dot_nn(sel, exp_ref[:, pl.ds(c0, kcs)])
        kpos = c0 + lax.broadcasted_iota(jnp.int32, (tq, kcs), 1)
        valid = jnp.where(kpos <= t, maskf, 0.0) > 0.5
        return tuple(_online_update(carry[g], jnp.where(valid, _dot_nt(qs[g], kblk), NEG), vblk)
                     for g in range(NSA_GROUP))

    init = tuple((jnp.full((tq, 1), NEG, F32), jnp.zeros((tq, 1), F32), jnp.zeros((tq, LANES), F32))
                 for _ in range(NSA_GROUP))
    fin = lax.fori_loop(0, n_chunks, chunk, init)

    ws = pl.multiple_of(jnp.maximum(t0 - NSA_WINDOW, 0), tq)
    kwin = kw_bf[pl.ds(ws, NSA_WCHUNK), :]
    vwin = vw_bf[pl.ds(ws, NSA_WCHUNK), :]
    dist = t - (ws + lax.broadcasted_iota(jnp.int32, (tq, NSA_WCHUNK), 1))
    wmask = (dist >= 0) & (dist < NSA_WINDOW)

    outs = []
    for g in range(NSA_GROUP):
        _, l_s, acc_s = fin[g]
        o_s = acc_s / l_s
        s = jnp.where(wmask, _dot_nt(qs[g], kwin), NEG)
        e = jnp.exp(s - jnp.max(s, axis=-1, keepdims=True))
        o_w = _dot_nn(e.astype(BF16), vwin) / jnp.sum(e, axis=-1, keepdims=True)
        outs.append(gate_col(g, 0) * o_c[g] + gate_col(g, 1) * o_s + gate_col(g, 2) * o_w)

    for j in range(NSA_GROUP // 2):
        a = jnp.where(hv == 0, outs[2 * j], _roll_half(outs[2 * j]))
        b = jnp.where(hv == 1, outs[2 * j + 1], _roll_half(outs[2 * j + 1]))
        o_ref[:, j * LANES:(j + 1) * LANES] = jnp.where(lo128, a, b).astype(o_ref.dtype)


def nsa_prompt_attention(q, gates, cmp, rows, win, n_b, seq):
    tq = NSA_TQ
    nq = seq // tq
    n_cmp_pad = seq // CMP_STRIDE
    assert n_cmp_pad == LANES and seq % NSA_KC == 0
    n_sel = seq // SEL_BLOCK
    smap = jnp.asarray(_sel_map(n_sel, n_cmp_pad - 1, n_cmp_pad, n_cmp_pad), BF16)
    expand = jnp.asarray((np.arange(n_cmp_pad)[:, None] == (np.arange(seq)[None, :] // SEL_BLOCK)), BF16)
    kvspec = lambda col0: pl.BlockSpec((seq, LANES), lambda b, k, i: (b, col0 + k // 2))
    return pl.pallas_call(
        _nsa_prompt_kernel,
        grid=(n_b, NSA_KV_HEADS, nq),
        in_specs=[pl.BlockSpec((tq, NSA_GROUP * HEAD_DIM), lambda b, k, i: (b * nq + i, k)),
                  pl.BlockSpec((tq, NSA_GATE_LANES), lambda b, k, i: (b * nq + i, 0)),
                  pl.BlockSpec((1, 1, n_cmp_pad, LANES), lambda b, k, i: (b, 0, 0, k // 2)),
                  pl.BlockSpec((1, 1, n_cmp_pad, LANES), lambda b, k, i: (b, 1, 0, k // 2)),
                  kvspec(4), kvspec(6), kvspec(0), kvspec(2),
                  pl.BlockSpec((n_cmp_pad, n_cmp_pad), lambda b, k, i: (0, 0)),
                  pl.BlockSpec((n_cmp_pad, seq), lambda b, k, i: (0, 0))],
        out_specs=pl.BlockSpec((tq, NSA_GROUP * HEAD_DIM), lambda b, k, i: (b * nq + i, k)),
        out_shape=jax.ShapeDtypeStruct((n_b * seq, NSA_QD), BF16),
        scratch_shapes=[pltpu.VMEM((seq, LANES), BF16)] * 4,
        compiler_params=_cparams(("arbitrary", "arbitrary", "arbitrary")),
        name="nsa_prompt_attention",
    )(q, gates, cmp, cmp, rows, rows, win, win, smap, expand)


HALVES_PER_PAGE = PAGE_SIZE // CMP_STRIDE
SEL_PER_PAGE = PAGE_SIZE // SEL_BLOCK
N_SEL_PAST = SEL_TOP - 1


def _stacked_first_layer(xs, w_ref, s):
    his = [x.astype(BF16).astype(F32) for x in xs]
    los_rolled = [_roll_half(x - h) for x, h in zip(xs, his)]
    is_lo_lane = _lane_is_lo(xs[0].shape)
    parts = []
    for k in range(NSA_KV_HEADS):
        ph = his[k // 2]
        plr = los_rolled[k // 2]
        if k % 2 == 0:
            col0, col1 = jnp.where(is_lo_lane, ph, plr), jnp.where(is_lo_lane, ph, 0.0)
        else:
            col0, col1 = jnp.where(is_lo_lane, plr, ph), jnp.where(is_lo_lane, 0.0, ph)
        lhs = jnp.concatenate([col0, col1], axis=-1).astype(BF16)
        parts.append(_dot_nn(lhs, w_ref[0, s, k]))
    return jnp.concatenate(parts, axis=-1)


def _cmp_sample_kernel(pt_ref, cache_ref, new_ref, w1_ref, b1_ref, w2h_ref, w2l_ref, b2_ref, kn_ref,
                       o_ref, buf, sem):
    b = pl.program_id(0)
    c = pl.program_id(1)
    n_pages = pt_ref.shape[1]
    n_half = n_pages * HALVES_PER_PAGE
    col0 = pl.multiple_of(c * NSA_KVD, NSA_KVD)

    n_pairs = NSA_KVD // LANES

    def page_copy(p, pr):
        return pltpu.make_async_copy(
            cache_ref.at[pt_ref[b, p], :, pl.ds(col0 + pr * LANES, LANES)],
            buf.at[pr, pl.ds(p * PAGE_SIZE, PAGE_SIZE)], sem)

    for p in range(n_pages):
        for pr in range(n_pairs):
            page_copy(p, pr).start()
    new_row = jnp.broadcast_to(new_ref[0], (ROWS8, NSA_KVD))
    t_all = _stacked_first_layer([new_row[:, pr * LANES:(pr + 1) * LANES] for pr in range(n_pairs)], w1_ref, 0)[0:1]
    extra = jnp.concatenate([t_all[:, k * 2 * PHI_HIDDEN + PHI_HIDDEN:(k + 1) * 2 * PHI_HIDDEN]
                             for k in range(NSA_KV_HEADS)], axis=-1)
    for p in range(n_pages):
        for pr in range(n_pairs):
            page_copy(p, pr).wait()

    def first_layer():
        acc = None
        for s in range(CMP_STRIDE):
            xs = [buf[pr, pl.ds(s, n_half, stride=CMP_STRIDE), :] for pr in range(n_pairs)]
            part = _stacked_first_layer(xs, w1_ref, s)
            acc = part if acc is None else acc + part
        return acc

    out = _compress_halves(first_layer, b1_ref[0],
                           lambda hid: _dot3_parts(_split2(hid), (w2h_ref[0], w2l_ref[0]), _dot_nn), b2_ref[0],
                           kn_ref[...], c == 0, extra_tail=extra)
    o_ref[0, 0] = out


def _stacked_cmp_weights(w1):
    lead_tail = jnp.concatenate([w1[:, :CMP_STRIDE], w1[:, CMP_STRIDE:]], axis=-1)
    w_hi, w_lo = _split2_masked(lead_tail)
    zero = jnp.zeros_like(w_hi)
    even = jnp.concatenate([w_hi, w_hi, w_lo, zero], axis=2)
    odd = jnp.concatenate([w_hi, w_hi, zero, w_lo], axis=2)
    return jnp.stack([even if k % 2 == 0 else odd for k in range(NSA_KV_HEADS)], axis=2)


def cmp_sample(cache2, page_ids, rows_new, k_cmp_norm, w1, b1, w2, b2):
    n_b, n_pages = page_ids.shape
    n_half = n_pages * HALVES_PER_PAGE
    new3 = rows_new.reshape(n_b, 1, 4 * NSA_KVD)
    _, b1t, w2bd, b2t = _cmp_weights(w1, b1, w2, b2)
    w1s = _stacked_cmp_weights(w1)
    w2h, w2l = _split2_masked(w2bd)
    kn = jnp.tile(k_cmp_norm, 2).reshape(1, LANES)
    w2spec = pl.BlockSpec((1, w2bd.shape[1], NSA_KVD), lambda b, c, pt: (c, 0, 0))
    grid_spec = pltpu.PrefetchScalarGridSpec(
        num_scalar_prefetch=1,
        grid=(n_b, 2),
        in_specs=[pl.BlockSpec(memory_space=pl.ANY),
                  pl.BlockSpec((1, 1, NSA_KVD), lambda b, c, pt: (b, 0, c)),
                  pl.BlockSpec((1,) + w1s.shape[1:], lambda b, c, pt: (c, 0, 0, 0, 0)),
                  pl.BlockSpec((1, 1, b1t.shape[-1]), lambda b, c, pt: (c, 0, 0)),
                  w2spec, w2spec,
                  pl.BlockSpec((1, 1, NSA_KVD), lambda b, c, pt: (c, 0, 0)),
                  pl.BlockSpec((1, LANES), lambda b, c, pt: (0, 0))],
        out_specs=pl.BlockSpec((1, 1, n_half, NSA_KVD), lambda b, c, pt: (b, c, 0, 0)),
        scratch_shapes=[pltpu.VMEM((NSA_KVD // LANES, n_pages * PAGE_SIZE, LANES), F32),
                        pltpu.SemaphoreType.DMA(())],
    )
    return pl.pallas_call(
        _cmp_sample_kernel,
        grid_spec=grid_spec,
        out_shape=jax.ShapeDtypeStruct((n_b, 2, n_half, NSA_KVD), F32),
        compiler_params=_cparams(("arbitrary", "arbitrary")),
        name="nsa_cmp_sample",
    )(page_ids, cache2, new3, w1s, b1t, w2h, w2l, b2t, kn)


def _group_queries(q_row, kv):
    lo = _lane_is_lo((ROWS8, LANES))
    row = lax.broadcasted_iota(jnp.int32, (ROWS8, LANES), 0)
    out = jnp.zeros((ROWS8, LANES), F32)
    for g in range(NSA_GROUP):
        h = kv * NSA_GROUP + g
        pair = jnp.broadcast_to(q_row[:, (h // 2) * LANES:(h // 2 + 1) * LANES], (ROWS8, LANES))
        own = jnp.where(lo if h % 2 == 0 else ~lo, pair, 0.0)
        own = own if h % 2 == kv % 2 else _roll_half(own)
        out = jnp.where(row == g, own, out)
    return out


def _sel_sample_kernel(q_ref, cmp_ref, map_ref, oc_ref, flag_ref, *, t_pos, n_sel):
    q_row = q_ref[0]
    n_cmp = cmp_ref.shape[2]
    pad = map_ref.shape[1]
    cmask = (lax.broadcasted_iota(jnp.int32, (ROWS8, n_cmp), 1) * CMP_STRIDE + (CMP_BLOCK - 1)) <= t_pos
    lane = lax.broadcasted_iota(jnp.int32, (1, pad), 1)
    cur = t_pos // SEL_BLOCK
    forced = (lane == 0) | (lane == cur) | (lane == cur - 1)
    eye = lax.broadcasted_iota(jnp.int32, (pad, pad), 0) == lax.broadcasted_iota(jnp.int32, (pad, pad), 1)
    before = lax.broadcasted_iota(jnp.int32, (pad, pad), 0) < lax.broadcasted_iota(jnp.int32, (pad, pad), 1)
    for kv in range(NSA_KV_HEADS):
        q8 = _group_queries(q_row, kv)
        kc = cmp_ref[0, 0, :, (kv // 2) * LANES:(kv // 2 + 1) * LANES]
        vc = cmp_ref[0, 1, :, (kv // 2) * LANES:(kv // 2 + 1) * LANES]
        s = jnp.where(cmask, _dot3_parts(_split2(q8), _split2(kc), _dot_nt), NEG)
        e = jnp.where(cmask, jnp.exp(s - jnp.max(s, axis=-1, keepdims=True)), 0.0)
        p = e / jnp.maximum(jnp.sum(e, axis=-1, keepdims=True), TINY)
        oc_ref[0, kv] = _dot3_parts(_split2(p), _split2(vc), _dot_nn)
        imp = p[0:1]
        for g in range(1, NSA_GROUP):
            imp = imp + p[g:g + 1]
        imp8 = jnp.broadcast_to(imp, (ROWS8, n_cmp))
        p_blk = None
        for term in _split3(imp8):
            part = _dot_nn(term, map_ref[...])
            p_blk = part if p_blk is None else p_blk + part
        score = jnp.where(lane * SEL_BLOCK > t_pos, NEG, jnp.where(forced, BIG, p_blk[0:1]))
        s_row = jnp.broadcast_to(score, (pad, pad))
        s_col = jnp.sum(jnp.where(eye, s_row, 0.0), axis=1, keepdims=True)
        beats = jnp.where(s_col > s_row, 1.0, jnp.where((s_col == s_row) & before, 1.0, 0.0))
        rank = jnp.sum(beats, axis=0, keepdims=True)
        chosen = (rank < min(SEL_TOP, n_sel)) & (lane < n_sel - 1)
        flag_ref[0, kv] = jnp.broadcast_to(jnp.where(chosen, 1, 0).astype(jnp.int32), (ROWS8, pad))


def sel_sample(q_s, cmp_s, t_pos):
    n_b = q_s.shape[0]
    n_cmp = cmp_s.shape[2]
    n_sel = -(-(t_pos + 1) // SEL_BLOCK)
    assert n_sel - 1 >= SEL_TOP
    pad = -(-n_sel // LANES) * LANES
    smap = jnp.asarray(_sel_map(n_sel, n_cmp, n_cmp, pad), BF16)
    kern = functools.partial(_sel_sample_kernel, t_pos=t_pos, n_sel=n_sel)
    return pl.pallas_call(
        kern,
        grid=(n_b,),
        in_specs=[pl.BlockSpec((1, 1, NSA_QD), lambda b: (b, 0, 0)),
                  pl.BlockSpec((1, 2, n_cmp, NSA_KVD), lambda b: (b, 0, 0, 0)),
                  pl.BlockSpec((n_cmp, pad), lambda b: (0, 0))],
        out_specs=[pl.BlockSpec((1, NSA_KV_HEADS, ROWS8, LANES), lambda b: (b, 0, 0, 0)),
                   pl.BlockSpec((1, NSA_KV_HEADS, ROWS8, pad), lambda b: (b, 0, 0, 0))],
        out_shape=[jax.ShapeDtypeStruct((n_b, NSA_KV_HEADS, ROWS8, LANES), F32),
                   jax.ShapeDtypeStruct((n_b, NSA_KV_HEADS, ROWS8, pad), jnp.int32)],
        compiler_params=_cparams(("parallel",)),
        name="nsa_sel_sample",
    )(q_s.reshape(n_b, 1, NSA_QD), cmp_s, smap)


def _attn_sample_kernel(pt_ref, flag_ref, q_ref, gate_ref, oc_ref, new_rows_ref, new_win_ref, wcache_ref, cache_ref,
                        o_ref, kbuf, vbuf, sem):
    b = pl.program_id(0)
    n_past_blocks = pt_ref.shape[1] * SEL_PER_PAGE

    for kv in range(NSA_KV_HEADS):
        def issue(j, slot):
            @pl.when(flag_ref[0, kv, j] != 0)
            def _():
                page = pt_ref[b, j // SEL_PER_PAGE]
                r0 = pl.multiple_of((j % SEL_PER_PAGE) * SEL_BLOCK, SEL_BLOCK)
                dst = jnp.minimum(slot, N_SEL_PAST - 1)
                pltpu.make_async_copy(cache_ref.at[page, pl.ds(r0, SEL_BLOCK), pl.ds((2 * 2 + kv // 2) * LANES, LANES)],
                                      kbuf.at[kv, dst], sem.at[kv]).start()
                pltpu.make_async_copy(cache_ref.at[page, pl.ds(r0, SEL_BLOCK), pl.ds((3 * 2 + kv // 2) * LANES, LANES)],
                                      vbuf.at[kv, dst], sem.at[kv]).start()
            return slot + jnp.where(flag_ref[0, kv, j] != 0, 1, 0)

        lax.fori_loop(0, n_past_blocks, issue, jnp.int32(0))

    q_row = q_ref[0]
    gates = jnp.broadcast_to(gate_ref[0], (ROWS8, NSA_GATE_LANES))
    grow = lax.broadcasted_iota(jnp.int32, gates.shape, 0)
    glane = lax.broadcasted_iota(jnp.int32, gates.shape, 1)
    lo = _lane_is_lo((1, LANES))
    new_rows = new_rows_ref[0]
    new_win = new_win_ref[0]
    wrow = lax.broadcasted_iota(jnp.int32, (ROWS8, wcache_ref.shape[1]), 1)

    outs = []
    for kv in range(NSA_KV_HEADS):
        pr = kv // 2
        q8 = _group_queries(q_row, kv)
        for _ in range(2 * N_SEL_PAST):
            pltpu.make_async_copy(cache_ref.at[0, pl.ds(0, SEL_BLOCK), pl.ds(0, LANES)], kbuf.at[kv, 0], sem.at[kv]).wait()
        keys = kbuf[kv].reshape(N_SEL_PAST * SEL_BLOCK, LANES)
        vals = vbuf[kv].reshape(N_SEL_PAST * SEL_BLOCK, LANES)
        num, den, _ = _attend_extra(q8, keys, vals, None,
                                    new_rows[:, (2 * 2 + pr) * LANES:(2 * 2 + pr + 1) * LANES],
                                    new_rows[:, (3 * 2 + pr) * LANES:(3 * 2 + pr + 1) * LANES])
        o_s = num / den
        num, den, _ = _attend_extra(q8, wcache_ref[0, :, pr * LANES:(pr + 1) * LANES],
                                    wcache_ref[0, :, (2 + pr) * LANES:(2 + pr + 1) * LANES], wrow >= 1,
                                    new_win[:, pr * LANES:(pr + 1) * LANES],
                                    new_win[:, (2 + pr) * LANES:(2 + pr + 1) * LANES])
        o_w = num / den

        def gate_col(c):
            idx = (kv * NSA_GROUP + grow) * 3 + c
            return jnp.sum(jnp.where((glane == idx) & (grow < NSA_GROUP), gates, 0.0), axis=-1, keepdims=True)

        outs.append(gate_col(0) * oc_ref[0, kv] + gate_col(1) * o_s + gate_col(2) * o_w)

    for kv in range(NSA_KV_HEADS):
        for jj in range(NSA_GROUP // 2):
            a = outs[kv][2 * jj:2 * jj + 1]
            bb = outs[kv][2 * jj + 1:2 * jj + 2]
            a = a if kv % 2 == 0 else _roll_half(a)
            bb = bb if kv % 2 == 1 else _roll_half(bb)
            col = (kv * NSA_GROUP + 2 * jj) // 2
            o_ref[0, :, col * LANES:(col + 1) * LANES] = jnp.where(lo, a, bb)


def attn_sample(q_s, gates_s, o_cmp, flags, rows_new, win_new, win_cache, cache2, page_ids):
    n_b = q_s.shape[0]
    n_win = win_cache.shape[1]
    assert n_win == NSA_WINDOW
    wc = win_cache.reshape(n_b, n_win, 2 * NSA_KVD)
    pad = flags.shape[-1]
    grid_spec = pltpu.PrefetchScalarGridSpec(
        num_scalar_prefetch=1,
        grid=(n_b,),
        in_specs=[pl.BlockSpec((1, NSA_KV_HEADS, pad), lambda b, pt: (b, 0, 0), memory_space=pltpu.SMEM),
                  pl.BlockSpec((1, 1, NSA_QD), lambda b, pt: (b, 0, 0)),
                  pl.BlockSpec((1, 1, NSA_GATE_LANES), lambda b, pt: (b, 0, 0)),
                  pl.BlockSpec((1, NSA_KV_HEADS, ROWS8, LANES), lambda b, pt: (b, 0, 0, 0)),
                  pl.BlockSpec((1, 1, 4 * NSA_KVD), lambda b, pt: (b, 0, 0)),
                  pl.BlockSpec((1, 1, 2 * NSA_KVD), lambda b, pt: (b, 0, 0)),
                  pl.BlockSpec((1, n_win, 2 * NSA_KVD), lambda b, pt: (b, 0, 0)),
                  pl.BlockSpec(memory_space=pl.ANY)],
        out_specs=pl.BlockSpec((1, 1, NSA_QD), lambda b, pt: (b, 0, 0)),
        scratch_shapes=[pltpu.VMEM((NSA_KV_HEADS, N_SEL_PAST, SEL_BLOCK, LANES), F32),
                        pltpu.VMEM((NSA_KV_HEADS, N_SEL_PAST, SEL_BLOCK, LANES), F32),
                        pltpu.SemaphoreType.DMA((NSA_KV_HEADS,))],
    )
    out = pl.pallas_call(
        _attn_sample_kernel,
        grid_spec=grid_spec,
        out_shape=jax.ShapeDtypeStruct((n_b, 1, NSA_QD), F32),
        compiler_params=_cparams(("arbitrary",)),
        name="nsa_attn_sample",
    )(page_ids, flags[:, :, 0, :], q_s.reshape(n_b, 1, NSA_QD), gates_s.reshape(n_b, 1, NSA_GATE_LANES), o_cmp,
      rows_new.reshape(n_b, 1, 4 * NSA_KVD), win_new.reshape(n_b, 1, 2 * NSA_KVD), wc, cache2)
    return out.reshape(n_b, NSA_QD)


DIL_QD = DIL_GROUPS * DIL_HEADS * HEAD_DIM
DIL_GD = DIL_HEADS * HEAD_DIM
DIL_TQ = 128
DIL_KC = 512


def _dil_proj_kernel(x_ref, g_ref, w_ref, qn_ref, kn_ref, q_ref, kv_ref, *, hp):
    y = _mm(_rms(x_ref[...], g_ref[...]), w_ref[...], hp)
    for p in range(DIL_QD // LANES):
        blk = y[:, p * LANES:(p + 1) * LANES]
        q_ref[:, p * LANES:(p + 1) * LANES] = (_pair_head_norm(blk, qn_ref[...]) * (HEAD_DIM ** -0.5)).astype(q_ref.dtype)
    for g in range(DIL_GROUPS):
        for p in range(DIL_GD // LANES):
            kblk = y[:, DIL_QD + g * DIL_GD + p * LANES:DIL_QD + g * DIL_GD + (p + 1) * LANES]
            vblk = y[:, 2 * DIL_QD + g * DIL_GD + p * LANES:2 * DIL_QD + g * DIL_GD + (p + 1) * LANES]
            kv_ref[:, g * 2 * DIL_GD + p * LANES:g * 2 * DIL_GD + (p + 1) * LANES] = _pair_head_norm(kblk, kn_ref[...])
            kv_ref[:, (g * 2 + 1) * DIL_GD + p * LANES:(g * 2 + 1) * DIL_GD + (p + 1) * LANES] = vblk


def dil_proj(x, g, w_in, q_norm, k_norm, hp=False):
    m, d = x.shape
    tm = _tile(m)
    qn = jnp.tile(q_norm, 2).reshape(1, LANES)
    kn = jnp.tile(k_norm, 2).reshape(1, LANES)
    return pl.pallas_call(
        functools.partial(_dil_proj_kernel, hp=hp),
        grid=(m // tm,),
        in_specs=[pl.BlockSpec((tm, d), lambda i: (i, 0)),
                  pl.BlockSpec((1, d), lambda i: (0, 0)),
                  pl.BlockSpec((d, 3 * DIL_QD), lambda i: (0, 0)),
                  pl.BlockSpec((1, LANES), lambda i: (0, 0)),
                  pl.BlockSpec((1, LANES), lambda i: (0, 0))],
        out_specs=[pl.BlockSpec((tm, DIL_QD), lambda i: (i, 0)),
                   pl.BlockSpec((tm, 2 * DIL_QD), lambda i: (i, 0))],
        out_shape=[jax.ShapeDtypeStruct((m, DIL_QD), _wdtype(hp)),
                   jax.ShapeDtypeStruct((m, 2 * DIL_QD), F32)],
        compiler_params=_cparams(("parallel",)),
        name="dil_proj",
    )(x, g.reshape(1, d), w_in.astype(_wdtype(hp)), qn, kn)


def _dil_prompt_kernel(q0_ref, q1_ref, q2_ref, k0_ref, v0_ref, k1_ref, v1_ref, k2_ref, v2_ref, o_ref,
                       k0_bf, v0_bf, k1_bf, v1_bf, k2_bf, v2_bf):
    i = pl.program_id(2)
    tq = DIL_TQ

    @pl.when(i == 0)
    def _():
        for src, dst in ((k0_ref, k0_bf), (v0_ref, v0_bf), (k1_ref, k1_bf), (v1_ref, v1_bf), (k2_ref, k2_bf), (v2_ref, v2_bf)):
            dst[...] = src[...].astype(BF16)

    t0 = i * tq
    t = t0 + lax.broadcasted_iota(jnp.int32, (tq, 1), 0)
    lo128 = _lane_is_lo((tq, LANES))
    q_refs = (q0_ref, q1_ref, q2_ref)
    k_bfs = (k0_bf, k1_bf, k2_bf)
    v_bfs = (v0_bf, v1_bf, v2_bf)

    def dil_mask(start, n, w, r):
        d = t - (start + lax.broadcasted_iota(jnp.int32, (tq, n), 1))
        return (d >= 0) & (d <= w) & ((d & (r - 1)) == 0)

    halves = []
    for hh in range(2):
        sel = lo128 if hh == 0 else ~lo128
        qh = [jnp.where(sel, q_refs[g][...].astype(F32), 0.0).astype(BF16) for g in range(DIL_GROUPS)]
        carry = (jnp.full((tq, 1), NEG, F32), jnp.zeros((tq, 1), F32), jnp.zeros((tq, LANES), F32))
        for g in range(DIL_GROUPS - 1):
            w, r = DIL_PAIRS[g]
            n = w + tq
            start = pl.multiple_of(jnp.maximum(t0 - w, 0), tq)
            s = jnp.where(dil_mask(start, n, w, r), _dot_nt(qh[g], k_bfs[g][pl.ds(start, n), :]), NEG)
            carry = _online_update(carry, s, v_bfs[g][pl.ds(start, n), :])
        w, r = DIL_PAIRS[DIL_GROUPS - 1]
        g = DIL_GROUPS - 1

        def chunk(c, carry):
            c0 = pl.multiple_of(c * DIL_KC, DIL_KC)
            s = jnp.where(dil_mask(c0, DIL_KC, w, r), _dot_nt(qh[g], k_bfs[g][pl.ds(c0, DIL_KC), :]), NEG)
            return _online_update(carry, s, v_bfs[g][pl.ds(c0, DIL_KC), :])

        _, l_f, acc_f = lax.fori_loop(0, (t0 + tq + DIL_KC - 1) // DIL_KC, chunk, carry)
        halves.append(acc_f / l_f)
    o_ref[...] = jnp.where(lo128, halves[0], halves[1]).astype(o_ref.dtype)


def dil_prompt_attention(q, kv, n_b, seq):
    tq = DIL_TQ
    nq = seq // tq
    n_pairs = DIL_GD // LANES
    assert DIL_PAIRS[-1][0] >= seq and seq % DIL_KC == 0
    qspec = lambda g: pl.BlockSpec((tq, LANES), lambda b, p, i: (b * nq + i, g * n_pairs + p))
    kvspec = lambda j: pl.BlockSpec((seq, LANES), lambda b, p, i: (b, j * n_pairs + p))
    return pl.pallas_call(
        _dil_prompt_kernel,
        grid=(n_b, n_pairs, nq),
        in_specs=[qspec(0), qspec(1), qspec(2)] + [kvspec(j) for j in range(2 * DIL_GROUPS)],
        out_specs=pl.BlockSpec((tq, LANES), lambda b, p, i: (b * nq + i, p)),
        out_shape=jax.ShapeDtypeStruct((n_b * seq, DIL_GD), BF16),
        scratch_shapes=[pltpu.VMEM((seq, LANES), BF16)] * (2 * DIL_GROUPS),
        compiler_params=_cparams(("arbitrary", "arbitrary", "arbitrary")),
        name="dil_prompt_attention",
    )(q, q, q, *([kv] * (2 * DIL_GROUPS)))


def _dil_sample_kernel(q_ref, new_ref, c0_ref, c1_ref, c2_ref, o_ref):
    q_row = q_ref[0]
    new_row = new_ref[0]
    caches = (c0_ref, c1_ref, c2_ref)
    lo = _lane_is_lo((ROWS8, LANES))
    row = lax.broadcasted_iota(jnp.int32, (ROWS8, LANES), 0)
    for p in range(DIL_GD // LANES):
        m_run = jnp.full((ROWS8, 1), NEG, F32)
        l_run = jnp.zeros((ROWS8, 1), F32)
        acc = jnp.zeros((ROWS8, LANES), F32)
        for g in range(DIL_GROUPS):
            qp = jnp.broadcast_to(q_row[:, g * DIL_GD + p * LANES:g * DIL_GD + (p + 1) * LANES], (ROWS8, LANES))
            q8 = jnp.where((row == 0) & lo, qp, jnp.where((row == 1) & ~lo, qp, 0.0))
            num, den, m_g = _attend_extra(
                q8, caches[g][0, :, p * LANES:(p + 1) * LANES], caches[g][0, :, DIL_GD + p * LANES:DIL_GD + (p + 1) * LANES],
                None,
                new_row[:, g * 2 * DIL_GD + p * LANES:g * 2 * DIL_GD + (p + 1) * LANES],
                new_row[:, (g * 2 + 1) * DIL_GD + p * LANES:(g * 2 + 1) * DIL_GD + (p + 1) * LANES])
            m_new = jnp.maximum(m_run, m_g)
            a_old = jnp.exp(m_run - m_new)
            a_g = jnp.exp(m_g - m_new)
            l_run = a_old * l_run + a_g * den
            acc = a_old * acc + a_g * num
            m_run = m_new
        o = acc / l_run
        o_ref[0, :, p * LANES:(p + 1) * LANES] = jnp.where(lo[0:1], o[0:1], o[1:2])


def dil_sample_attention(q_s, kv_new, caches):
    n_b = q_s.shape[0]
    views = []
    for c, (w, r) in zip(caches, DIL_PAIRS):
        assert c.shape[1] == w
        views.append(c.reshape(n_b, w // r, r * 2 * DIL_GD))
    cspec = lambda w, r: pl.BlockSpec((1, w // r, 2 * DIL_GD), lambda b: (b, 0, 0))
    out = pl.pallas_call(
        _dil_sample_kernel,
        grid=(n_b,),
        in_specs=[pl.BlockSpec((1, 1, DIL_QD), lambda b: (b, 0, 0)),
                  pl.BlockSpec((1, 1, 2 * DIL_QD), lambda b: (b, 0, 0))] + [cspec(w, r) for w, r in DIL_PAIRS],
        out_specs=pl.BlockSpec((1, 1, DIL_GD), lambda b: (b, 0, 0)),
        out_shape=jax.ShapeDtypeStruct((n_b, 1, DIL_GD), F32),
        compiler_params=_cparams(("parallel",)),
        name="dil_sample_attention",
    )(q_s.reshape(n_b, 1, DIL_QD), kv_new.reshape(n_b, 1, 2 * DIL_QD), *views)
    return out.reshape(n_b, DIL_GD)


def kernel(x_prompt, x_sample, cache_nsa_kv, cache_nsa_win, state_lru_conv, state_lru_h, cache_dil_win0, cache_dil_win1, cache_dil_win2, page_table, norm_mix, norm_ffn, nsa_w_in, nsa_q_norm, nsa_k_norm, nsa_cmp_w1, nsa_cmp_b1, nsa_cmp_w2, nsa_cmp_b2, nsa_w_out, lru_w_in, lru_conv_w, lru_conv_b, lru_gate_a_w, lru_gate_a_b, lru_gate_x_w, lru_gate_x_b, lru_lambda, lru_w_out, dil_w_in, dil_q_norm, dil_k_norm, dil_w_out, moe_router_group_w, moe_router_group_b, moe_router_expert_w, moe_router_expert_b, moe_w_in, moe_w_out):
    depth = norm_mix.shape[0]
    n_pr, seq, d = x_prompt.shape
    n_dec, dec_seq, _ = x_sample.shape
    assert dec_seq == 1
    past_len = page_table.shape[1] * PAGE_SIZE
    dil_caches = (cache_dil_win0, cache_dil_win1, cache_dil_win2)
    n_p = n_pr * seq
    xp = x_prompt.reshape(n_p, d)
    xs = x_sample.reshape(n_dec, d)
    n_pool = cache_nsa_kv.shape[1]
    cache2 = cache_nsa_kv.reshape(cache_nsa_kv.shape[0] * n_pool, PAGE_SIZE, 4 * NSA_KVD)

    kv_p, kv_s, nw_p, nw_s = [], [], [], []
    cv_p, cv_s, hh_p, hh_s = [], [], [], []
    dw_p, dw_s = ([], [], []), ([], [], [])
    for layer in range(depth):
        j = layer // N_MIXERS
        if layer % N_MIXERS == 0:
            phi = (nsa_k_norm[j][0], nsa_cmp_w1[j], nsa_cmp_b1[j], nsa_cmp_w2[j], nsa_cmp_b2[j])
            q, gates, rows, win = nsa_proj(xp, norm_mix[layer], nsa_w_in[j], nsa_q_norm[j], nsa_k_norm[j])
            cmp = cmp_prompt(rows, n_pr, seq, *phi)
            op = nsa_prompt_attention(q, gates, cmp, rows, win, n_pr, seq)
            xp = matmul_res(op, nsa_w_out[j], xp)
            q_s, g_s, rows_s, win_s = nsa_proj(xs, norm_mix[layer], nsa_w_in[j], nsa_q_norm[j], nsa_k_norm[j], hp=True)
            page_ids = page_table + j * n_pool
            cmp_s = cmp_sample(cache2, page_ids, rows_s, *phi)
            o_cmp, flags = sel_sample(q_s, cmp_s, past_len)
            os_ = attn_sample(q_s, g_s, o_cmp, flags, rows_s, win_s, cache_nsa_win[j], cache2, page_ids)
            xs = matmul_res(os_, nsa_w_out[j], xs, hp=True)
            kv_p.append(rows.reshape(n_pr, seq, 4, NSA_KV_HEADS, HEAD_DIM))
            kv_s.append(rows_s.reshape(n_dec, dec_seq, 4, NSA_KV_HEADS, HEAD_DIM))
            nw_p.append(win.reshape(n_pr, seq, 2, NSA_KV_HEADS, HEAD_DIM)[:, seq - min(NSA_WINDOW, seq):])
            nw_s.append(jnp.concatenate([cache_nsa_win[j][:, dec_seq:],
                                         win_s.reshape(n_dec, dec_seq, 2, NSA_KV_HEADS, HEAD_DIM)], axis=1))
        elif layer % N_MIXERS == 1:
            lp = (lru_conv_w[j], lru_conv_b[j], lru_gate_a_w[j], lru_gate_a_b[j],
                  lru_gate_x_w[j], lru_gate_x_b[j], lru_lambda[j])
            xy_p = norm_proj(xp, norm_mix[layer], lru_w_in[j]).reshape(n_pr, seq, 2 * D_RNN)
            op, h_p = lru_prompt(xy_p, *lp)
            xp = matmul_res(op.reshape(n_p, D_RNN), lru_w_out[j], xp)
            xy_s = norm_proj(xs, norm_mix[layer], lru_w_in[j], hp=True)
            os_, h_s = lru_step(xy_s, jnp.swapaxes(state_lru_conv[j], 0, 1), state_lru_h[j], *lp)
            xs = matmul_res(os_, lru_w_out[j], xs, hp=True)
            cv_p.append(xy_p[:, seq - (CONV_W - 1):, :D_RNN])
            cv_s.append(jnp.concatenate([state_lru_conv[j], xy_s[:, None, :D_RNN]], axis=1)[:, -(CONV_W - 1):])
            hh_p.append(h_p[:, 0])
            hh_s.append(h_s)
        else:
            q, kv = dil_proj(xp, norm_mix[layer], dil_w_in[j], dil_q_norm[j], dil_k_norm[j])
            op = dil_prompt_attention(q, kv, n_pr, seq)
            xp = matmul_res(op, dil_w_out[j], xp)
            q_s, kv_s_new = dil_proj(xs, norm_mix[layer], dil_w_in[j], dil_q_norm[j], dil_k_norm[j], hp=True)
            os_ = dil_sample_attention(q_s, kv_s_new, [c[j] for c in dil_caches])
            xs = matmul_res(os_, dil_w_out[j], xs, hp=True)
            kvp = kv.reshape(n_pr, seq, DIL_GROUPS, 2, DIL_HEADS, HEAD_DIM)
            kvs = kv_s_new.reshape(n_dec, dec_seq, DIL_GROUPS, 2, DIL_HEADS, HEAD_DIM)
            for g, (w, _) in enumerate(DIL_PAIRS):
                dw_p[g].append(kvp[:, seq - min(w, seq):, g])
                dw_s[g].append(jnp.concatenate([dil_caches[g][j][:, dec_seq:], kvs[:, :, g]], axis=1))
        moe = (norm_ffn[layer], moe_router_group_w[layer], moe_router_group_b[layer],
               moe_router_expert_w[layer], moe_router_expert_b[layer], moe_w_in[layer], moe_w_out[layer])
        xp = hier_moe_res(xp, *moe)
        xs = hier_moe_res(xs, *moe, hp=True)
    return (xp.reshape(n_pr, seq, d), xs.reshape(n_dec, dec_seq, d),
            jnp.stack(kv_p), jnp.stack(kv_s), jnp.stack(nw_p), jnp.stack(nw_s),
            jnp.stack(cv_p), jnp.stack(cv_s), jnp.stack(hh_p), jnp.stack(hh_s),
            jnp.stack(dw_p[0]), jnp.stack(dw_s[0]), jnp.stack(dw_p[1]), jnp.stack(dw_s[1]),
            jnp.stack(dw_p[2]), jnp.stack(dw_s[2]))
```
